```python
import jax, jax.numpy as jnp
from jax import lax
import numpy as np

D_MODEL = 1024
BATCH = 16
SEQ = 2048
DEPTH = 2
DEC_BATCH = 32
DEC_SEQ = 2048
PAST_LEN = 128

N_MIXERS = 2
N_FNET_LAYERS = (DEPTH + 1) // 2
N_LRU_LAYERS = DEPTH // 2
FNET_GROUPS = 4
FNET_GROUP_WIDTH = D_MODEL // FNET_GROUPS
D_RNN = 1280
LRU_BLOCKS = 10
LRU_BLOCK = D_RNN // LRU_BLOCKS
LRU_C = 8.0
LRU_CONV = 4
LRU_PAD_L, LRU_PAD_R = 2, 1
N_MEM = 256
XA_HEADS = 4
XA_HEAD_DIM = D_MODEL // XA_HEADS
D_FF = 2816
FFN_CONV = 3
EPS = 1e-6

kernel_name = "hybrid_fnet_rglru_xattn_encoder"


def _rmsnorm(x, g):
    xf = x.astype(jnp.float32)
    y = xf * lax.rsqrt(jnp.mean(xf * xf, axis=-1, keepdims=True) + EPS)
    return (y * g.astype(jnp.float32)).astype(x.dtype)


def _dwconv(x, w, b, pad_l, pad_r):
    s = x.shape[1]
    xp = jnp.pad(x, ((0, 0), (pad_l, pad_r), (0, 0)))
    out = xp[:, 0:s] * w[0]
    for k in range(1, w.shape[0]):
        out = out + xp[:, k:k + s] * w[k]
    return out + b


def _fourier_mixer(xn, w_out, b_out):
    bsz, s, d = xn.shape
    xg = xn.reshape(bsz, s, FNET_GROUPS, FNET_GROUP_WIDTH).astype(jnp.float32)
    f = jnp.fft.fft2(xg, axes=(1, 3), norm="ortho").real
    f = f.reshape(bsz, s, d).astype(xn.dtype)
    return f @ w_out + b_out


def _lin_combine(c1, c2):
    a1, b1 = c1
    a2, b2 = c2
    return a1 * a2, a2 * b1 + b2


def _rglru_mixer(xn, w_in, conv_w, conv_b, w_a, b_a, w_i, b_i, lam, w_out):
    bsz, s, _ = xn.shape
    u = xn @ w_in
    gate_br, rec_br = jnp.split(u, 2, axis=-1)
    gate = jax.nn.gelu(gate_br)
    c = _dwconv(rec_br, conv_w, conv_b, LRU_PAD_L, LRU_PAD_R)
    cb = c.reshape(bsz, s, LRU_BLOCKS, LRU_BLOCK)
    r = jax.nn.sigmoid((jnp.einsum('bshi,ehij->ebshj', cb, w_a).reshape(2, bsz, s, D_RNN)
                        + b_a[:, None, None, :]).astype(jnp.float32))
    ig = jax.nn.sigmoid((jnp.einsum('bshi,ehij->ebshj', cb, w_i).reshape(2, bsz, s, D_RNN)
                         + b_i[:, None, None, :]).astype(jnp.float32))
    log_a = -LRU_C * jax.nn.softplus(-lam.astype(jnp.float32))[:, None, None, :] * r
    a = jnp.exp(log_a)
    bterm = jnp.sqrt(-jnp.expm1(2.0 * log_a)) * (ig * c.astype(jnp.float32)[None])
    _, h_fwd = lax.associative_scan(_lin_combine, (a[0], bterm[0]), axis=1)
    _, h_bwd = lax.associative_scan(_lin_combine, (a[1], bterm[1]), axis=1, reverse=True)
    h = (h_fwd + h_bwd).astype(xn.dtype)
    return (h * gate) @ w_out


def _cross_attention(xn, mn, w_q, w_kv, w_o):
    bsz, s, d = xn.shape
    m = mn.shape[1]
    q = (xn @ w_q).reshape(bsz, s, XA_HEADS, XA_HEAD_DIM)
    k, v = jnp.split(mn @ w_kv, 2, axis=-1)
    k = k.reshape(bsz, m, XA_HEADS, XA_HEAD_DIM)
    v = v.reshape(bsz, m, XA_HEADS, XA_HEAD_DIM)
    scores = jnp.einsum('bshk,bmhk->bhsm', q, k).astype(jnp.float32) * (XA_HEAD_DIM ** -0.5)
    p = jax.nn.softmax(scores, axis=-1).astype(v.dtype)
    o = jnp.einsum('bhsm,bmhk->bshk', p, v).reshape(bsz, s, d)
    return o @ w_o


def _conv_ffn(xn, w_up, conv_w, conv_b, w_down):
    g, v = jnp.split(xn @ w_up, 2, axis=-1)
    g = _dwconv(g, conv_w, conv_b, FFN_CONV // 2, FFN_CONV // 2)
    return (jax.nn.gelu(g) * v) @ w_down


def _trunk(x, mem, p):
    for i in range(DEPTH):
        xn = _rmsnorm(x, p['norm_mix'][i])
        if i % N_MIXERS == 0:
            j = i // N_MIXERS
            x = x + _fourier_mixer(xn, p['fnet_w_out'][j], p['fnet_b_out'][j])
        else:
            j = i // N_MIXERS
            x = x + _rglru_mixer(xn, p['lru_w_in'][j], p['lru_conv_w'][j], p['lru_conv_b'][j],
                                 p['lru_w_a'][j], p['lru_b_a'][j], p['lru_w_i'][j], p['lru_b_i'][j],
                                 p['lru_lambda'][j], p['lru_w_out'][j])
        xn = _rmsnorm(x, p['norm_xa'][i])
        mn = _rmsnorm(mem, p['norm_mem'][i])
        x = x + _cross_attention(xn, mn, p['xa_w_q'][i], p['xa_w_kv'][i], p['xa_w_o'][i])
        xn = _rmsnorm(x, p['norm_ffn'][i])
        x = x + _conv_ffn(xn, p['ffn_w_up'][i], p['ffn_conv_w'][i], p['ffn_conv_b'][i], p['ffn_w_down'][i])
    return _rmsnorm(x, p['norm_final'])


def setup_inputs(seed: int = 0) -> dict:
    key = jax.random.key(seed)
    ks = jax.random.split(key, 32)
    f32 = jnp.float32

    def nrm(k, shape, fan_in):
        return jax.random.normal(k, shape, f32) * (fan_in ** -0.5)

    def gain(k, shape):
        return 1.0 + 0.02 * jax.random.normal(k, shape, f32)

    def small(k, shape):
        return 0.01 * jax.random.normal(k, shape, f32)

    u = jax.random.uniform(ks[12], (N_LRU_LAYERS, 2, D_RNN), f32, 0.9, 0.999)
    a0 = u ** (1.0 / LRU_C)
    lam = jnp.log(a0) - jnp.log1p(-a0)
    return {
        'x_prompt': jax.random.normal(ks[0], (BATCH, SEQ, D_MODEL), f32),
        'x_sample': jax.random.normal(ks[1], (DEC_BATCH, DEC_SEQ, D_MODEL), f32),
        'mem_prompt': jax.random.normal(ks[2], (BATCH, N_MEM, D_MODEL), f32),
        'mem_sample': jax.random.normal(ks[3], (DEC_BATCH, N_MEM, D_MODEL), f32),
        'norm_mix': gain(ks[4], (DEPTH, D_MODEL)),
        'fnet_w_out': nrm(ks[5], (N_FNET_LAYERS, D_MODEL, D_MODEL), D_MODEL),
        'fnet_b_out': small(ks[6], (N_FNET_LAYERS, D_MODEL)),
        'lru_w_in': nrm(ks[7], (N_LRU_LAYERS, D_MODEL, 2 * D_RNN), D_MODEL),
        'lru_conv_w': nrm(ks[8], (N_LRU_LAYERS, LRU_CONV, D_RNN), LRU_CONV),
        'lru_conv_b': small(ks[9], (N_LRU_LAYERS, D_RNN)),
        'lru_w_a': nrm(ks[10], (N_LRU_LAYERS, 2, LRU_BLOCKS, LRU_BLOCK, LRU_BLOCK), LRU_BLOCK),
        'lru_b_a': small(ks[11], (N_LRU_LAYERS, 2, D_RNN)),
        'lru_w_i': nrm(ks[13], (N_LRU_LAYERS, 2, LRU_BLOCKS, LRU_BLOCK, LRU_BLOCK), LRU_BLOCK),
        'lru_b_i': small(ks[14], (N_LRU_LAYERS, 2, D_RNN)),
        'lru_lambda': lam,
        'lru_w_out': nrm(ks[15], (N_LRU_LAYERS, D_RNN, D_MODEL), D_RNN),
        'norm_xa': gain(ks[16], (DEPTH, D_MODEL)),
        'norm_mem': gain(ks[17], (DEPTH, D_MODEL)),
        'xa_w_q': nrm(ks[18], (DEPTH, D_MODEL, D_MODEL), D_MODEL),
        'xa_w_kv': nrm(ks[19], (DEPTH, D_MODEL, 2 * D_MODEL), D_MODEL),
        'xa_w_o': nrm(ks[20], (DEPTH, D_MODEL, D_MODEL), D_MODEL),
        'norm_ffn': gain(ks[21], (DEPTH, D_MODEL)),
        'ffn_w_up': nrm(ks[22], (DEPTH, D_MODEL, 2 * D_FF), D_MODEL),
        'ffn_conv_w': nrm(ks[23], (DEPTH, FFN_CONV, D_FF), FFN_CONV),
        'ffn_conv_b': small(ks[24], (DEPTH, D_FF)),
        'ffn_w_down': nrm(ks[25], (DEPTH, D_FF, D_MODEL), D_FF),
        'norm_final': gain(ks[26], (D_MODEL,)),
    }


def reference(x_prompt, x_sample, mem_prompt, mem_sample, norm_mix, fnet_w_out, fnet_b_out,
              lru_w_in, lru_conv_w, lru_conv_b, lru_w_a, lru_b_a, lru_w_i, lru_b_i, lru_lambda,
              lru_w_out, norm_xa, norm_mem, xa_w_q, xa_w_kv, xa_w_o, norm_ffn, ffn_w_up,
              ffn_conv_w, ffn_conv_b, ffn_w_down, norm_final):
    params = {
        'norm_mix': norm_mix, 'fnet_w_out': fnet_w_out, 'fnet_b_out': fnet_b_out,
        'lru_w_in': lru_w_in, 'lru_conv_w': lru_conv_w, 'lru_conv_b': lru_conv_b,
        'lru_w_a': lru_w_a, 'lru_b_a': lru_b_a, 'lru_w_i': lru_w_i, 'lru_b_i': lru_b_i,
        'lru_lambda': lru_lambda, 'lru_w_out': lru_w_out,
        'norm_xa': norm_xa, 'norm_mem': norm_mem, 'xa_w_q': xa_w_q, 'xa_w_kv': xa_w_kv, 'xa_w_o': xa_w_o,
        'norm_ffn': norm_ffn, 'ffn_w_up': ffn_w_up, 'ffn_conv_w': ffn_conv_w, 'ffn_conv_b': ffn_conv_b,
        'ffn_w_down': ffn_w_down, 'norm_final': norm_final,
    }
    y_prompt = _trunk(x_prompt, mem_prompt, params)
    y_sample = _trunk(x_sample, mem_sample, params)
    return (y_prompt, y_sample)
```

```python
import functools

import numpy as np
import jax
import jax.numpy as jnp
from jax import lax
from jax.experimental import pallas as pl
from jax.experimental.pallas import tpu as pltpu

F32 = jnp.float32
BF16 = jnp.bfloat16

D_MODEL = 1024
EPS = 1e-6
FNET_GROUPS = 4
FNET_W = D_MODEL // FNET_GROUPS
FNET_RADIX = 4
D_RNN = 1280
LRU_BLOCK = 128
LRU_CB = 256
LRU_C = 8.0
LRU_SEGS = 8
N_MEM = 256
XA_HEADS = 4
XA_HEAD_DIM = D_MODEL // XA_HEADS
D_FF = 2816
FFN_FC = 256
FFN_HALO = 16

LANES = 128
V7X_VMEM_LIMIT_BYTES =56 * 1024 * 1024


def _rms(x, g):
    ms = jnp.mean(x * x, axis=-1, keepdims=True)
    return (x * lax.rsqrt(ms + EPS)) * g


def _dot(a, b):
    return jnp.dot(a, b, preferred_element_type=F32)


def _params(*sem):
    return pltpu.CompilerParams(dimension_semantics=sem, vmem_limit_bytes=V7X_VMEM_LIMIT_BYTES)


def _resmm_bias_kernel(x_ref, y_ref, w_ref, b_ref, o_ref):
    o_ref[...] = x_ref[...] + (_dot(y_ref[...], w_ref[...]) + b_ref[...])


def _resmm_kernel(x_ref, y_ref, w_ref, o_ref):
    o_ref[...] = x_ref[...] + _dot(y_ref[...], w_ref[...])


def _residual_matmul(x, y, w, b=None, tm=512):
    bsz, s, d = x.shape
    k = y.shape[-1]
    rows = bsz * s
    tm = min(tm, rows)
    x2 = x.reshape(rows, d)
    y2 = y.reshape(rows, k)
    in_specs = [
        pl.BlockSpec((tm, d), lambda i: (i, 0)),
        pl.BlockSpec((tm, k), lambda i: (i, 0)),
        pl.BlockSpec((k, d), lambda i: (0, 0)),
    ]
    args = [x2, y2, w]
    body = _resmm_kernel
    if b is not None:
        in_specs.append(pl.BlockSpec((1, d), lambda i: (0, 0)))
        args.append(b.reshape(1, d))
        body = _resmm_bias_kernel
    out = pl.pallas_call(
        body,
        out_shape=jax.ShapeDtypeStruct((rows, d), F32),
        grid=(rows // tm,),
        in_specs=in_specs,
        out_specs=pl.BlockSpec((tm, d), lambda i: (i, 0)),
        compiler_params=_params("parallel"),
        name="residual_matmul",
    )(*args)
    return out.reshape(bsz, s, d)


def _fnet_kernel(x_ref, g_ref, cw_ref, m_ref, f_ref, xn_scr, u_scr, y_scr, *, seq):
    gi = pl.program_id(1)
    q = seq // FNET_RADIX

    @pl.when(gi == 0)
    def _():
        rows = min(512, seq)
        for r0 in range(0, seq, rows):
            xn = _rms(x_ref[0, r0:r0 + rows, :], g_ref[...]).astype(BF16)
            for g in range(FNET_GROUPS):
                xn_scr[g, r0:r0 + rows, :] = xn[:, g * FNET_W:(g + 1) * FNET_W]

    z = _dot(xn_scr[gi], cw_ref[...])
    zr = [z[j * q:(j + 1) * q, :FNET_W] for j in range(4)]
    zi = [z[j * q:(j + 1) * q, FNET_W:] for j in range(4)]
    t0r, t0i = zr[0] + zr[2], zi[0] + zi[2]
    t1r, t1i = zr[0] - zr[2], zi[0] - zi[2]
    t2r, t2i = zr[1] + zr[3], zi[1] + zi[3]
    t3r, t3i = zr[1] - zr[3], zi[1] - zi[3]
    u = [(t0r + t2r, t0i + t2i), (t1r + t3i, t1i - t3r),
         (t0r - t2r, t0i - t2i), (t1r - t3i, t1i + t3r)]
    for r in range(4):
        u_scr[r, :q, :] = u[r][0].astype(BF16)
        u_scr[r, q:, :] = u[r][1].astype(BF16)
    for r in range(4):
        yr = _dot(m_ref[r], u_scr[r])
        for h in range(FNET_W // LANES):
            y_scr.at[h][pl.ds(r, q, stride=FNET_RADIX), :] = yr[:, h * LANES:(h + 1) * LANES]
    for h in range(FNET_W // LANES):
        f_ref[0, :, h * LANES:(h + 1) * LANES] = y_scr[h].astype(BF16)


def _fnet_tables(seq):
    w = FNET_W
    cw_idx = np.outer(np.arange(w), np.arange(w)) % w
    ang = 2.0 * np.pi * cw_idx / w
    cw = np.concatenate([np.cos(ang), -np.sin(ang)], axis=1) / np.sqrt(w)
    q = seq // FNET_RADIX
    mats = []
    for r in range(FNET_RADIX):
        kn = np.outer(FNET_RADIX * np.arange(q) + r, np.arange(q)) % seq
        ang = 2.0 * np.pi * kn / seq
        mats.append(np.concatenate([np.cos(ang), np.sin(ang)], axis=1) / np.sqrt(seq))
    m = np.stack(mats)
    return cw.astype(np.float32), m.astype(np.float32)


def _fnet_mix(x, gain, cw, m):
    bsz, s, d = x.shape
    q = s // FNET_RADIX
    return pl.pallas_call(
        functools.partial(_fnet_kernel, seq=s),
        out_shape=jax.ShapeDtypeStruct((bsz, s, d), BF16),
        grid=(bsz, FNET_GROUPS),
        in_specs=[
            pl.BlockSpec((1, s, d), lambda b, g: (b, 0, 0)),
            pl.BlockSpec((1, d), lambda b, g: (0, 0)),
            pl.BlockSpec((FNET_W, 2 * FNET_W), lambda b, g: (0, 0)),
            pl.BlockSpec((FNET_RADIX, q, 2 * q), lambda b, g: (0, 0, 0)),
        ],
        out_specs=pl.BlockSpec((1, s, FNET_W), lambda b, g: (b, 0, g)),
        scratch_shapes=[
            pltpu.VMEM((FNET_GROUPS, s, FNET_W), BF16),
            pltpu.VMEM((FNET_RADIX, 2 * q, FNET_W), BF16),
            pltpu.VMEM((FNET_W // LANES, s, LANES), F32),
        ],
        compiler_params=_params("parallel", "arbitrary"),
        name="fnet_dft",
    )(x, gain.reshape(1, d), cw, m)


def _lru_kernel(x_ref, g_ref, wgate_ref, wrec_ref, cw_ref, cb_ref, wg_ref, bg_ref, nls_ref,
                y_ref, xn_scr, rec_scr, ab_scr, yp_scr, *, seq):
    ci = pl.program_id(1)
    seg = seq // LRU_SEGS
    cb_w = LRU_CB
    nsub = cb_w // LRU_BLOCK
    rc = min(512, seq)

    def a_tile(d, sub):
        return ab_scr.at[d * nsub + sub]

    def b_tile(d, sub):
        return ab_scr.at[2 * nsub + d * nsub + sub]

    @pl.when(ci == 0)
    def _():
        for i in range(LRU_SEGS):
            xn = _rms(x_ref[0, i * seg:(i + 1) * seg, :], g_ref[...])
            for t in range(D_MODEL // LANES):
                ab_scr.at[t][pl.ds(i, seg, stride=LRU_SEGS), :] = xn[:, t * LANES:(t + 1) * LANES]
        for t in range(D_MODEL // LANES):
            xn_scr[:, t * LANES:(t + 1) * LANES] = ab_scr[t].astype(BF16)

    rec_scr[...] = _dot(xn_scr[...], wrec_ref[...])

    def edge_down(blk):
        row = lax.broadcasted_iota(jnp.int32, blk.shape, 0)
        return jnp.where(row == 0, 0.0, pltpu.roll(blk, 1, 0))

    def edge_up(blk):
        row = lax.broadcasted_iota(jnp.int32, blk.shape, 0)
        return jnp.where(row == LRU_SEGS - 1, 0.0, pltpu.roll(blk, LRU_SEGS - 1, 0))

    def shifted(k, r0):
        if k > 0:
            lo = r0 - 8 * k
            if lo >= 0:
                return rec_scr[lo:lo + rc, :]
            heads = [edge_down(rec_scr[seq - 8 * (k - e):seq - 8 * (k - e) + 8, :]) for e in range(k)]
            return jnp.concatenate(heads + [rec_scr[0:rc - 8 * k, :]], axis=0)
        hi = r0 + 8
        if hi + rc <= seq:
            return rec_scr[hi:hi + rc, :]
        return jnp.concatenate([rec_scr[hi:seq, :], edge_up(rec_scr[0:8, :])], axis=0)

    cw = cw_ref[...]
    for r0 in range(0, seq, rc):
        c = (shifted(2, r0) * cw[0:1] + shifted(1, r0) * cw[1:2]
             + rec_scr[r0:r0 + rc, :] * cw[2:3] + shifted(-1, r0) * cw[3:4]) + cb_ref[...]
        c16 = c.astype(BF16)
        for sub in range(cb_w // LRU_BLOCK):
            lo = sub * LRU_BLOCK
            csub = c[:, lo:lo + LRU_BLOCK]
            gts = _dot(c16[:, lo:lo + LRU_BLOCK], wg_ref[sub]) + bg_ref[sub]
            nls = nls_ref[sub]
            for d in range(2):
                r = jax.nn.sigmoid(gts[:, d * LRU_BLOCK:(d + 1) * LRU_BLOCK])
                ig = jax.nn.sigmoid(gts[:, (2 + d) * LRU_BLOCK:(3 + d) * LRU_BLOCK])
                log_a = nls[:, d * LRU_BLOCK:(d + 1) * LRU_BLOCK] * r
                a = jnp.exp(log_a)
                one_m_a2 = -jnp.tanh(log_a) * (1.0 + a * a)
                a_tile(d, sub)[r0:r0 + rc, :] = a
                b_tile(d, sub)[r0:r0 + rc, :] = jnp.sqrt(one_m_a2) * (ig * csub)

    chains = [(d, sub) for d in range(2) for sub in range(nsub)]

    def scan_step(j, carry):
        new = []
        for (d, sub), (h, p) in zip(chains, carry):
            row = pl.multiple_of((j if d == 0 else seg - 1 - j) * 8, 8)
            a = a_tile(d, sub)[pl.ds(row, 8), :]
            h = a * h + b_tile(d, sub)[pl.ds(row, 8), :]
            p = a * p
            b_tile(d, sub)[pl.ds(row, 8), :] = h
            a_tile(d, sub)[pl.ds(row, 8), :] = p
            new.append((h, p))
        return tuple(new)

    zeros = jnp.zeros((8, LANES), F32)
    ones = jnp.ones((8, LANES), F32)
    ends = lax.fori_loop(0, seg, scan_step, tuple((zeros, ones) for _ in chains))

    carries = {}
    for (d, sub), (h_end, p_end) in zip(chains, ends):
        move = edge_down if d == 0 else edge_up
        e, q = move(h_end), move(p_end)
        cin = jnp.zeros((8, LANES), F32)
        for _ in range(LRU_SEGS - 1):
            cin = e + q * move(cin)
        carries[(d, sub)] = cin

    for r0 in range(0, seq, rc):
        reps = rc // 8
        gate = jax.nn.gelu(_dot(xn_scr[r0:r0 + rc, :], wgate_ref[...]))
        for sub in range(nsub):
            hf = (b_tile(0, sub)[r0:r0 + rc, :]
                  + a_tile(0, sub)[r0:r0 + rc, :] * jnp.tile(carries[(0, sub)], (reps, 1)))
            hb = (b_tile(1, sub)[r0:r0 + rc, :]
                  + a_tile(1, sub)[r0:r0 + rc, :] * jnp.tile(carries[(1, sub)], (reps, 1)))
            yp_scr.at[sub][r0:r0 + rc, :] = (hf + hb) * gate[:, sub * LANES:(sub + 1) * LANES]

    for i in range(LRU_SEGS):
        for sub in range(nsub):
            y_ref[0, i * seg:(i + 1) * seg, sub * LANES:(sub + 1) * LANES] = (
                yp_scr.at[sub][pl.ds(i, seg, stride=LRU_SEGS), :].astype(BF16))


def _lru_mix(x, gain, w_in, conv_w, conv_b, wg, bg, nls):
    bsz, s, d = x.shape
    nb = D_RNN // LRU_CB
    sub = LRU_CB // LRU_BLOCK
    assert d // LANES >= 4 * sub and LRU_BLOCK == LANES
    return pl.pallas_call(
        functools.partial(_lru_kernel, seq=s),
        out_shape=jax.ShapeDtypeStruct((bsz, s, D_RNN), BF16),
        grid=(bsz, nb),
        in_specs=[
            pl.BlockSpec((1, s, d), lambda b, c: (b, 0, 0)),
            pl.BlockSpec((1, d), lambda b, c: (0, 0)),
            pl.BlockSpec((d, LRU_CB), lambda b, c: (0, c)),
            pl.BlockSpec((d, LRU_CB), lambda b, c: (0, nb + c)),
            pl.BlockSpec((4, LRU_CB), lambda b, c: (0, c)),
            pl.BlockSpec((1, LRU_CB), lambda b, c: (0, c)),
            pl.BlockSpec((sub, LRU_BLOCK, 4 * LRU_BLOCK), lambda b, c: (c, 0, 0)),
            pl.BlockSpec((sub, 1, 4 * LRU_BLOCK), lambda b, c: (c, 0, 0)),
            pl.BlockSpec((sub, 1, 2 * LRU_BLOCK), lambda b, c: (c, 0, 0)),
        ],
        out_specs=pl.BlockSpec((1, s, LRU_CB), lambda b, c: (b, 0, c)),
        scratch_shapes=[
            pltpu.VMEM((s, d), BF16),
            pltpu.VMEM((s, LRU_CB), F32),
            pltpu.VMEM((d // LANES, s, LANES), F32),
            pltpu.VMEM((sub, s, LANES), F32),
        ],
        compiler_params=_params("parallel", "arbitrary"),
        name="lru_core",
    )(x, gain.reshape(1, d), w_in, w_in, conv_w, conv_b.reshape(1, D_RNN), wg, bg, nls)


def _kv_kernel(m_ref, g_ref, w_ref, k_ref, v_ref):
    mn = _rms(m_ref[...], g_ref[...]).astype(BF16)
    kv = _dot(mn, w_ref[...])
    k_ref[...] = kv[:, :D_MODEL].astype(BF16)
    v_ref[...] = kv[:, D_MODEL:].astype(BF16)


def _kv_proj(mem, gain, w_kv, tm=512):
    bsz, m, d = mem.shape
    rows = bsz * m
    tm = min(tm, rows)
    k, v = pl.pallas_call(
        _kv_kernel,
        out_shape=(jax.ShapeDtypeStruct((rows, d), BF16), jax.ShapeDtypeStruct((rows, d), BF16)),
        grid=(rows // tm,),
        in_specs=[
            pl.BlockSpec((tm, d), lambda i: (i, 0)),
            pl.BlockSpec((1, d), lambda i: (0, 0)),
            pl.BlockSpec((d, 2 * d), lambda i: (0, 0)),
        ],
        out_specs=(pl.BlockSpec((tm, d), lambda i: (i, 0)), pl.BlockSpec((tm, d), lambda i: (i, 0))),
        compiler_params=_params("parallel"),
        name="kv_proj",
    )(mem.reshape(rows, d), gain.reshape(1, d), w_kv)
    return k.reshape(bsz, m, d), v.reshape(bsz, m, d)


def _xattn_kernel(x_ref, g_ref, k_ref, v_ref, wq_ref, wo_ref, o_ref, oh_scr):
    x = x_ref[0]
    xn = _rms(x, g_ref[...]).astype(BF16)
    q = _dot(xn, wq_ref[...])
    for h in range(XA_HEADS):
        lo = h * XA_HEAD_DIM
        qh = q[:, lo:lo + XA_HEAD_DIM].astype(BF16)
        kh = k_ref[0, :, lo:lo + XA_HEAD_DIM]
        s = lax.dot_general(qh, kh, (((1,), (1,)), ((), ())), preferred_element_type=F32)
        s = s * (XA_HEAD_DIM ** -0.5)
        e = jnp.exp(s - jnp.max(s, axis=-1, keepdims=True))
        p = e / jnp.sum(e, axis=-1, keepdims=True)
        oh = _dot(p.astype(BF16), v_ref[0, :, lo:lo + XA_HEAD_DIM])
        oh_scr[:, lo:lo + XA_HEAD_DIM] = oh.astype(BF16)
    o_ref[0] = x + _dot(oh_scr[...], wo_ref[...])


def _cross_attention(x, gain, k, v, w_q, w_o, tq=512):
    bsz, s, d = x.shape
    m = k.shape[1]
    tq = min(tq, s)
    return pl.pallas_call(
        _xattn_kernel,
        out_shape=jax.ShapeDtypeStruct((bsz, s, d), F32),
        grid=(bsz, s // tq),
        in_specs=[
            pl.BlockSpec((1, tq, d), lambda b, i: (b, i, 0)),
            pl.BlockSpec((1, d), lambda b, i: (0, 0)),
            pl.BlockSpec((1, m, d), lambda b, i: (b, 0, 0)),
            pl.BlockSpec((1, m, d), lambda b, i: (b, 0, 0)),
            pl.BlockSpec((d, d), lambda b, i: (0, 0)),
            pl.BlockSpec((d, d), lambda b, i: (0, 0)),
        ],
        out_specs=pl.BlockSpec((1, tq, d), lambda b, i: (b, i, 0)),
        scratch_shapes=[pltpu.VMEM((tq, d), BF16)],
        compiler_params=_params("parallel", "parallel"),
        name="cross_attention",
    )(x, gain.reshape(1, d), k, v, w_q, w_o)


def _ffn_kernel(xp_ref, x_ref, xq_ref, g_ref, wup_ref, cw_ref, cb_ref, wdn_ref, gf_ref, o_ref,
                xe_scr, h_scr, *, tf, final_norm):
    i = pl.program_id(1)
    last = pl.num_programs(1) - 1
    g = g_ref[...]
    ext = tf + 2 * FFN_HALO
    x = x_ref[0]
    xe_scr[0:FFN_HALO, :] = jnp.where(i > 0, _rms(xp_ref[0], g), 0.0).astype(BF16)
    xe_scr[FFN_HALO:FFN_HALO + tf, :] = _rms(x, g).astype(BF16)
    xe_scr[FFN_HALO + tf:ext, :] = jnp.where(i < last, _rms(xq_ref[0], g), 0.0).astype(BF16)
    for c in range(D_FF // FFN_FC):
        lo = c * FFN_FC
        ge = _dot(xe_scr[...], wup_ref[:, lo:lo + FFN_FC])
        val = _dot(xe_scr[FFN_HALO:FFN_HALO + tf, :], wup_ref[:, D_FF + lo:D_FF + lo + FFN_FC])
        gm1 = pltpu.roll(ge, 1, 0)[FFN_HALO:FFN_HALO + tf]
        g0 = ge[FFN_HALO:FFN_HALO + tf]
        gp1 = pltpu.roll(ge, ext - 1, 0)[FFN_HALO:FFN_HALO + tf]
        w = cw_ref[:, lo:lo + FFN_FC]
        gc = (gm1 * w[0:1] + g0 * w[1:2] + gp1 * w[2:3]) + cb_ref[:, lo:lo + FFN_FC]
        h_scr[:, lo:lo + FFN_FC] = (jax.nn.gelu(gc) * val).astype(BF16)
    y = x + _dot(h_scr[...], wdn_ref[...])
    if final_norm:
        y = _rms(y, gf_ref[...])
    o_ref[0] = y


def _conv_ffn(x, gain, w_up, conv_w, conv_b, w_down, final_gain, final_norm, tf=512):
    bsz, s, d = x.shape
    tf = min(tf, s)
    nt = s // tf
    hpt = tf // FFN_HALO
    nhb = s // FFN_HALO
    const = dict(pipeline_mode=pl.Buffered(1))
    return pl.pallas_call(
        functools.partial(_ffn_kernel, tf=tf, final_norm=final_norm),
        out_shape=jax.ShapeDtypeStruct((bsz, s, d), F32),
        grid=(bsz, nt),
        in_specs=[
            pl.BlockSpec((1, FFN_HALO, d), lambda b, i: (b, jnp.maximum(i * hpt - 1, 0), 0)),
            pl.BlockSpec((1, tf, d), lambda b, i: (b, i, 0)),
            pl.BlockSpec((1, FFN_HALO, d), lambda b, i: (b, jnp.minimum((i + 1) * hpt, nhb - 1), 0)),
            pl.BlockSpec((1, d), lambda b, i: (0, 0)),
            pl.BlockSpec((d, 2 * D_FF), lambda b, i: (0, 0), **const),
            pl.BlockSpec((3, D_FF), lambda b, i: (0, 0)),
            pl.BlockSpec((1, D_FF), lambda b, i: (0, 0)),
            pl.BlockSpec((D_FF, d), lambda b, i: (0, 0), **const),
            pl.BlockSpec((1, d), lambda b, i: (0, 0)),
        ],
        out_specs=pl.BlockSpec((1, tf, d), lambda b, i: (b, i, 0)),
        scratch_shapes=[
            pltpu.VMEM((tf + 2 * FFN_HALO, d), BF16),
            pltpu.VMEM((tf, D_FF), BF16),
        ],
        compiler_params=_params("parallel", "parallel"),
        name="conv_ffn",
    )(x, x, x, gain.reshape(1, d), w_up, conv_w, conv_b.reshape(1, D_FF), w_down,
      final_gain.reshape(1, d))


def _prepare(p):
    q = dict(p)
    for name in ("fnet_w_out", "lru_w_in", "lru_w_out", "xa_w_q", "xa_w_kv", "xa_w_o",
                 "ffn_w_up", "ffn_w_down"):
        q[name] = p[name].astype(BF16)
    nblk = D_RNN // LRU_BLOCK
    w_a, w_i = p["lru_w_a"], p["lru_w_i"]
    q["lru_wg"] = jnp.concatenate([w_a[:, 0], w_a[:, 1], w_i[:, 0], w_i[:, 1]], axis=-1).astype(BF16)

    def per_block(v):
        v = v.reshape(v.shape[0], 2, nblk, LRU_BLOCK)
        return jnp.concatenate([v[:, 0], v[:, 1]], axis=-1)[:, :, None, :]

    q["lru_bg"] = jnp.concatenate([per_block(p["lru_b_a"]), per_block(p["lru_b_i"])], axis=-1)
    q["lru_nls"] = per_block(-LRU_C * jax.nn.softplus(-p["lru_lambda"].astype(F32)))
    return q


def _trunk(x, mem, p, tables):
    depth = p["norm_mix"].shape[0]
    cw, m = tables
    for i in range(depth):
        j = i // 2
        if i % 2 == 0:
            f = _fnet_mix(x, p["norm_mix"][i], cw, m)
            x = _residual_matmul(x, f, p["fnet_w_out"][j], p["fnet_b_out"][j])
        else:
            y = _lru_mix(x, p["norm_mix"][i], p["lru_w_in"][j], p["lru_conv_w"][j], p["lru_conv_b"][j],
                         p["lru_wg"][j], p["lru_bg"][j], p["lru_nls"][j])
            x = _residual_matmul(x, y, p["lru_w_out"][j])
        k, v = _kv_proj(mem, p["norm_mem"][i], p["xa_w_kv"][i])
        x = _cross_attention(x, p["norm_xa"][i], k, v, p["xa_w_q"][i], p["xa_w_o"][i])
        x = _conv_ffn(x, p["norm_ffn"][i], p["ffn_w_up"][i], p["ffn_conv_w"][i], p["ffn_conv_b"][i],
                      p["ffn_w_down"][i], p["norm_final"], final_norm=(i == depth - 1))
    return x


def kernel(x_prompt, x_sample, mem_prompt, mem_sample, norm_mix, fnet_w_out, fnet_b_out, lru_w_in, lru_conv_w, lru_conv_b, lru_w_a, lru_b_a, lru_w_i, lru_b_i, lru_lambda, lru_w_out, norm_xa, norm_mem, xa_w_q, xa_w_kv, xa_w_o, norm_ffn, ffn_w_up, ffn_conv_w, ffn_conv_b, ffn_w_down, norm_final):
    params = _prepare({
        "norm_mix": norm_mix, "fnet_w_out": fnet_w_out, "fnet_b_out": fnet_b_out,
        "lru_w_in": lru_w_in, "lru_conv_w": lru_conv_w, "lru_conv_b": lru_conv_b,
        "lru_w_a": lru_w_a, "lru_b_a": lru_b_a, "lru_w_i": lru_w_i, "lru_b_i": lru_b_i,
        "lru_lambda": lru_lambda, "lru_w_out": lru_w_out,
        "norm_xa": norm_xa, "norm_mem": norm_mem, "xa_w_q": xa_w_q, "xa_w_kv": xa_w_kv, "xa_w_o": xa_w_o,
        "norm_ffn": norm_ffn, "ffn_w_up": ffn_w_up, "ffn_conv_w": ffn_conv_w, "ffn_conv_b": ffn_conv_b,
        "ffn_w_down": ffn_w_down, "norm_final": norm_final,
    })
    outs = []
    for x, mem in ((x_prompt, mem_prompt), (x_sample, mem_sample)):
        cw, m = _fnet_tables(x.shape[1])
        tables = (jnp.asarray(cw).astype(BF16), jnp.asarray(m).astype(BF16))
        outs.append(_trunk(x, mem, params, tables))
    return tuple(outs)
```

```python
import functools

import numpy as np
import jax
import jax.numpy as jnp
from jax import lax
from jax.experimental import pallas as pl
from jax.experimental.pallas import tpu as pltpu

F32 = jnp.float32
BF16 = jnp.bfloat16

D_MODEL = 1024
EPS = 1e-6
FNET_GROUPS = 4
FNET_W = D_MODEL // FNET_GROUPS
FNET_RADIX = 4
D_RNN = 1280
LRU_BLOCK = 128
LRU_CB = 256
LRU_C = 8.0
LRU_SEGS = 8
LRU_STEPS = 8
F32_TINY = float(np.finfo(np.float32).tiny)
N_MEM = 256
XA_HEADS = 4
XA_HEAD_DIM = D_MODEL // XA_HEADS
D_FF = 2816
FFN_FC = 256
FFN_HALO = 16

LANES = 128
V7X_VMEM_LIMIT_BYTES =56 * 1024 * 1024


def _rms(x, g):
    ms = jnp.mean(x * x, axis=-1, keepdims=True)
    return (x * lax.rsqrt(ms + EPS)) * g


def _dot(a, b):
    return jnp.dot(a, b, preferred_element_type=F32)


def _params(*sem):
    return pltpu.CompilerParams(dimension_semantics=sem, vmem_limit_bytes=V7X_VMEM_LIMIT_BYTES)


def _fnet_kernel(x_ref, g_ref, cw_ref, m_ref, f_ref, xn_scr, u_scr, y_scr, *, seq):
    gi = pl.program_id(1)
    q = seq // FNET_RADIX

    @pl.when(gi == 0)
    def _():
        rows = min(512, seq)
        for r0 in range(0, seq, rows):
            xn = _rms(x_ref[0, r0:r0 + rows, :], g_ref[...]).astype(BF16)
            for g in range(FNET_GROUPS):
                xn_scr[g, r0:r0 + rows, :] = xn[:, g * FNET_W:(g + 1) * FNET_W]

    z = _dot(xn_scr[gi], cw_ref[...])
    zr = [z[j * q:(j + 1) * q, :FNET_W] for j in range(4)]
    zi = [z[j * q:(j + 1) * q, FNET_W:] for j in range(4)]
    t0r, t0i = zr[0] + zr[2], zi[0] + zi[2]
    t1r, t1i = zr[0] - zr[2], zi[0] - zi[2]
    t2r, t2i = zr[1] + zr[3], zi[1] + zi[3]
    t3r, t3i = zr[1] - zr[3], zi[1] - zi[3]
    u = [(t0r + t2r, t0i + t2i), (t1r + t3i, t1i - t3r),
         (t0r - t2r, t0i - t2i), (t1r - t3i, t1i + t3r)]
    for r in range(4):
        u_scr[r, :q, :] = u[r][0].astype(BF16)
        u_scr[r, q:, :] = u[r][1].astype(BF16)
    for r in range(4):
        yr = _dot(m_ref[r], u_scr[r])
        for h in range(FNET_W // LANES):
            y_scr.at[h][pl.ds(r, q, stride=FNET_RADIX), :] = yr[:, h * LANES:(h + 1) * LANES]
    for h in range(FNET_W // LANES):
        f_ref[0, :, h * LANES:(h + 1) * LANES] = y_scr[h].astype(BF16)


def _fnet_tables(seq):
    w = FNET_W
    cw_idx = np.outer(np.arange(w), np.arange(w)) % w
    ang = 2.0 * np.pi * cw_idx / w
    cw = np.concatenate([np.cos(ang), -np.sin(ang)], axis=1) / np.sqrt(w)
    q = seq // FNET_RADIX
    mats = []
    for r in range(FNET_RADIX):
        kn = np.outer(FNET_RADIX * np.arange(q) + r, np.arange(q)) % seq
        ang = 2.0 * np.pi * kn / seq
        mats.append(np.concatenate([np.cos(ang), np.sin(ang)], axis=1) / np.sqrt(seq))
    m = np.stack(mats)
    return cw.astype(np.float32), m.astype(np.float32)


def _fnet_mix(x, gain, cw, m):
    bsz, s, d = x.shape
    q = s // FNET_RADIX
    return pl.pallas_call(
        functools.partial(_fnet_kernel, seq=s),
        out_shape=jax.ShapeDtypeStruct((bsz, s, d), BF16),
        grid=(bsz, FNET_GROUPS),
        in_specs=[
            pl.BlockSpec((1, s, d), lambda b, g: (b, 0, 0)),
            pl.BlockSpec((1, d), lambda b, g: (0, 0)),
            pl.BlockSpec((FNET_W, 2 * FNET_W), lambda b, g: (0, 0)),
            pl.BlockSpec((FNET_RADIX, q, 2 * q), lambda b, g: (0, 0, 0)),
        ],
        out_specs=pl.BlockSpec((1, s, FNET_W), lambda b, g: (b, 0, g)),
        scratch_shapes=[
            pltpu.VMEM((FNET_GROUPS, s, FNET_W), BF16),
            pltpu.VMEM((FNET_RADIX, 2 * q, FNET_W), BF16),
            pltpu.VMEM((FNET_W // LANES, s, LANES), F32),
        ],
        compiler_params=_params("parallel", "arbitrary"),
        name="fnet_dft",
    )(x, gain.reshape(1, d), cw, m)


def _lru_kernel(x_ref, g_ref, wgate_ref, wrec_ref, cw_ref, cb_ref, wg_ref, bg_ref, nls_ref,
                y_ref, xn_scr, rec_scr, ch_scr, gts_scr, hp_scr, yp_scr, *, seq):
    ci = pl.program_id(1)
    seg = seq // LRU_SEGS
    cb_w = LRU_CB
    nsub = cb_w // LRU_BLOCK
    rc = min(512, seq)

    def p_tile(d, sub):
        return hp_scr.at[d * nsub + sub]

    def h_tile(d, sub):
        return hp_scr.at[2 * nsub + d * nsub + sub]

    @pl.when(ci == 0)
    def _():
        for i in range(LRU_SEGS):
            xn = _rms(x_ref[0, i * seg:(i + 1) * seg, :], g_ref[...])
            for t in range(D_MODEL // LANES):
                hp_scr.at[t][pl.ds(i, seg, stride=LRU_SEGS), :] = xn[:, t * LANES:(t + 1) * LANES]
        for t in range(D_MODEL // LANES):
            xn_scr[:, t * LANES:(t + 1) * LANES] = hp_scr[t].astype(BF16)

    rec_scr[...] = _dot(xn_scr[...], wrec_ref[...])

    def edge_down(blk):
        row = lax.broadcasted_iota(jnp.int32, blk.shape, 0)
        return jnp.where(row == 0, 0.0, pltpu.roll(blk, 1, 0))

    def edge_up(blk):
        row = lax.broadcasted_iota(jnp.int32, blk.shape, 0)
        return jnp.where(row == LRU_SEGS - 1, 0.0, pltpu.roll(blk, LRU_SEGS - 1, 0))

    def shifted(k, r0):
        if k > 0:
            lo = r0 - 8 * k
            if lo >= 0:
                return rec_scr[lo:lo + rc, :]
            heads = [edge_down(rec_scr[seq - 8 * (k - e):seq - 8 * (k - e) + 8, :]) for e in range(k)]
            return jnp.concatenate(heads + [rec_scr[0:rc - 8 * k, :]], axis=0)
        hi = r0 + 8
        if hi + rc <= seq:
            return rec_scr[hi:hi + rc, :]
        return jnp.concatenate([rec_scr[hi:seq, :], edge_up(rec_scr[0:8, :])], axis=0)

    cw = cw_ref[...]
    for r0 in range(0, seq, rc):
        c = (shifted(2, r0) * cw[0:1] + shifted(1, r0) * cw[1:2]
             + rec_scr[r0:r0 + rc, :] * cw[2:3] + shifted(-1, r0) * cw[3:4]) + cb_ref[...]
        ch_scr[r0:r0 + rc, :] = 0.5 * c
        c16 = c.astype(BF16)
        for sub in range(nsub):
            lo = sub * LRU_BLOCK
            gts_scr[sub, r0:r0 + rc, :] = _dot(c16[:, lo:lo + LRU_BLOCK], wg_ref[sub])

    chains = [(d, sub) for d in range(2) for sub in range(nsub)]
    blk_rows = 8 * LRU_STEPS
    nblk = seg // LRU_STEPS

    def scan_block(it, carry):
        new = []
        for (d, sub), (h, p) in zip(chains, carry):
            row0 = pl.multiple_of((it if d == 0 else nblk - 1 - it) * blk_rows, blk_rows)
            rows = pl.ds(row0, blk_rows)
            ta = jnp.tanh(gts_scr[sub, rows, d * LANES:(d + 1) * LANES]
                          + bg_ref[sub, :, d * LANES:(d + 1) * LANES])
            ti = jnp.tanh(gts_scr[sub, rows, (2 + d) * LANES:(3 + d) * LANES]
                          + bg_ref[sub, :, (2 + d) * LANES:(3 + d) * LANES])
            nlsh = nls_ref[sub, :, d * LANES:(d + 1) * LANES]
            log_a = nlsh * ta + nlsh
            a = jnp.exp(log_a)
            z = jnp.tanh(log_a) * (-1.0 - a * a)
            ch = ch_scr[rows, sub * LANES:(sub + 1) * LANES]
            b = (z * lax.rsqrt(jnp.maximum(z, F32_TINY))) * (ch * ti + ch)
            hs, ps = [None] * LRU_STEPS, [None] * LRU_STEPS
            for k in (range(LRU_STEPS) if d == 0 else reversed(range(LRU_STEPS))):
                ak = a[8 * k:8 * k + 8]
                h = ak * h + b[8 * k:8 * k + 8]
                p = ak * p
                hs[k], ps[k] = h, p
            h_tile(d, sub)[rows, :] = jnp.concatenate(hs, axis=0)
            p_tile(d, sub)[rows, :] = jnp.concatenate(ps, axis=0)
            new.append((h, p))
        return tuple(new)

    zeros = jnp.zeros((8, LANES), F32)
    ones = jnp.ones((8, LANES), F32)
    ends = lax.fori_loop(0, nblk, scan_block, tuple((zeros, ones) for _ in chains))

    carries = {}
    for (d, sub), (h_end, p_end) in zip(chains, ends):
        move = edge_down if d == 0 else edge_up
        e, q = move(h_end), move(p_end)
        cin = jnp.zeros((8, LANES), F32)
        for _ in range(LRU_SEGS - 1):
            cin = e + q * move(cin)
        carries[(d, sub)] = cin

    for r0 in range(0, seq, rc):
        reps = rc // 8
        gate = jax.nn.gelu(_dot(xn_scr[r0:r0 + rc, :], wgate_ref[...]))
        for sub in range(nsub):
            hf = (h_tile(0, sub)[r0:r0 + rc, :]
                  + p_tile(0, sub)[r0:r0 + rc, :] * jnp.tile(carries[(0, sub)], (reps, 1)))
            hb = (h_tile(1, sub)[r0:r0 + rc, :]
                  + p_tile(1, sub)[r0:r0 + rc, :] * jnp.tile(carries[(1, sub)], (reps, 1)))
            yp_scr.at[sub][r0:r0 + rc, :] = (hf + hb) * gate[:, sub * LANES:(sub + 1) * LANES]

    for i in range(LRU_SEGS):
        for sub in range(nsub):
            y_ref[0, i * seg:(i + 1) * seg, sub * LANES:(sub + 1) * LANES] = (
                yp_scr.at[sub][pl.ds(i, seg, stride=LRU_SEGS), :].astype(BF16))


def _lru_mix(x, gain, w_in, conv_w, conv_b, wg, bg, nls):
    bsz, s, d = x.shape
    nb = D_RNN // LRU_CB
    sub = LRU_CB // LRU_BLOCK
    assert d // LANES >= 4 * sub and LRU_BLOCK == LANES
    return pl.pallas_call(
        functools.partial(_lru_kernel, seq=s),
        out_shape=jax.ShapeDtypeStruct((bsz, s, D_RNN), BF16),
        grid=(bsz, nb),
        in_specs=[
            pl.BlockSpec((1, s, d), lambda b, c: (b, 0, 0)),
            pl.BlockSpec((1, d), lambda b, c: (0, 0)),
            pl.BlockSpec((d, LRU_CB), lambda b, c: (0, c)),
            pl.BlockSpec((d, LRU_CB), lambda b, c: (0, nb + c)),
            pl.BlockSpec((4, LRU_CB), lambda b, c: (0, c)),
            pl.BlockSpec((1, LRU_CB), lambda b, c: (0, c)),
            pl.BlockSpec((sub, LRU_BLOCK, 4 * LRU_BLOCK), lambda b, c: (c, 0, 0)),
            pl.BlockSpec((sub, 1, 4 * LRU_BLOCK), lambda b, c: (c, 0, 0)),
            pl.BlockSpec((sub, 1, 2 * LRU_BLOCK), lambda b, c: (c, 0, 0)),
        ],
        out_specs=pl.BlockSpec((1, s, LRU_CB), lambda b, c: (b, 0, c)),
        scratch_shapes=[
            pltpu.VMEM((s, d), BF16),
            pltpu.VMEM((s, LRU_CB), F32),
            pltpu.VMEM((s, LRU_CB), F32),
            pltpu.VMEM((sub, s, 4 * LRU_BLOCK), F32),
            pltpu.VMEM((d // LANES, s, LANES), F32),
            pltpu.VMEM((sub, s, LANES), F32),
        ],
        compiler_params=_params("parallel", "arbitrary"),
        name="lru_core",
    )(x, gain.reshape(1, d), w_in, w_in, conv_w, conv_b.reshape(1, D_RNN), wg, bg, nls)


def _kv_kernel(m_ref, g_ref, w_ref, k_ref, v_ref):
    mn = _rms(m_ref[...], g_ref[...]).astype(BF16)
    kv = _dot(mn, w_ref[...])
    k_ref[...] = kv[:, :D_MODEL].astype(BF16)
    v_ref[...] = kv[:, D_MODEL:].astype(BF16)


def _kv_proj(mem, gain, w_kv, tm=512):
    bsz, m, d = mem.shape
    rows = bsz * m
    tm = min(tm, rows)
    k, v = pl.pallas_call(
        _kv_kernel,
        out_shape=(jax.ShapeDtypeStruct((rows, d), BF16), jax.ShapeDtypeStruct((rows, d), BF16)),
        grid=(rows // tm,),
        in_specs=[
            pl.BlockSpec((tm, d), lambda i: (i, 0)),
            pl.BlockSpec((1, d), lambda i: (0, 0)),
            pl.BlockSpec((d, 2 * d), lambda i: (0, 0)),
        ],
        out_specs=(pl.BlockSpec((tm, d), lambda i: (i, 0)), pl.BlockSpec((tm, d), lambda i: (i, 0))),
        compiler_params=_params("parallel"),
        name="kv_proj",
    )(mem.reshape(rows, d), gain.reshape(1, d), w_kv)
    return k.reshape(bsz, m, d), v.reshape(bsz, m, d)


def _xattn_kernel(x_ref, y_ref, wm_ref, bm_ref, g_ref, k_ref, v_ref, wq_ref, wo_ref, o_ref, oh_scr,
                  *, mixer_bias):
    mix = _dot(y_ref[0], wm_ref[...])
    if mixer_bias:
        mix = mix + bm_ref[...]
    x = x_ref[0] + mix
    xn = _rms(x, g_ref[...]).astype(BF16)
    q = _dot(xn, wq_ref[...])
    for h in range(XA_HEADS):
        lo = h * XA_HEAD_DIM
        qh = q[:, lo:lo + XA_HEAD_DIM].astype(BF16)
        kh = k_ref[0, :, lo:lo + XA_HEAD_DIM]
        s = lax.dot_general(qh, kh, (((1,), (1,)), ((), ())), preferred_element_type=F32)
        s = s * (XA_HEAD_DIM ** -0.5)
        e = jnp.exp(s - jnp.max(s, axis=-1, keepdims=True))
        p = e / jnp.sum(e, axis=-1, keepdims=True)
        oh = _dot(p.astype(BF16), v_ref[0, :, lo:lo + XA_HEAD_DIM])
        oh_scr[:, lo:lo + XA_HEAD_DIM] = oh.astype(BF16)
    o_ref[0] = x + _dot(oh_scr[...], wo_ref[...])


def _mixer_out_cross_attention(x, y, w_mix, b_mix, gain, k, v, w_q, w_o, tq=512):
    bsz, s, d = x.shape
    m = k.shape[1]
    ky = y.shape[-1]
    tq = min(tq, s)
    mixer_bias = b_mix is not None
    if not mixer_bias:
        b_mix = jnp.zeros((d,), F32)
    return pl.pallas_call(
        functools.partial(_xattn_kernel, mixer_bias=mixer_bias),
        out_shape=jax.ShapeDtypeStruct((bsz, s, d), F32),
        grid=(bsz, s // tq),
        in_specs=[
            pl.BlockSpec((1, tq, d), lambda b, i: (b, i, 0)),
            pl.BlockSpec((1, tq, ky), lambda b, i: (b, i, 0)),
            pl.BlockSpec((ky, d), lambda b, i: (0, 0)),
            pl.BlockSpec((1, d), lambda b, i: (0, 0)),
            pl.BlockSpec((1, d), lambda b, i: (0, 0)),
            pl.BlockSpec((1, m, d), lambda b, i: (b, 0, 0)),
            pl.BlockSpec((1, m, d), lambda b, i: (b, 0, 0)),
            pl.BlockSpec((d, d), lambda b, i: (0, 0)),
            pl.BlockSpec((d, d), lambda b, i: (0, 0)),
        ],
        out_specs=pl.BlockSpec((1, tq, d), lambda b, i: (b, i, 0)),
        scratch_shapes=[pltpu.VMEM((tq, d), BF16)],
        compiler_params=_params("parallel", "parallel"),
        name="cross_attention",
    )(x, y, w_mix, b_mix.reshape(1, d), gain.reshape(1, d), k, v, w_q, w_o)


def _ffn_kernel(xp_ref, x_ref, xq_ref, g_ref, wup_ref, cw_ref, cb_ref, wdn_ref, gf_ref, o_ref,
                xe_scr, h_scr, *, tf, final_norm):
    i = pl.program_id(1)
    last = pl.num_programs(1) - 1
    g = g_ref[...]
    ext = tf + 2 * FFN_HALO
    x = x_ref[0]
    xe_scr[0:FFN_HALO, :] = jnp.where(i > 0, _rms(xp_ref[0], g), 0.0).astype(BF16)
    xe_scr[FFN_HALO:FFN_HALO + tf, :] = _rms(x, g).astype(BF16)
    xe_scr[FFN_HALO + tf:ext, :] = jnp.where(i < last, _rms(xq_ref[0], g), 0.0).astype(BF16)
    for c in range(D_FF // FFN_FC):
        lo = c * FFN_FC
        ge = _dot(xe_scr[...], wup_ref[:, lo:lo + FFN_FC])
        val = _dot(xe_scr[FFN_HALO:FFN_HALO + tf, :], wup_ref[:, D_FF + lo:D_FF + lo + FFN_FC])
        gm1 = pltpu.roll(ge, 1, 0)[FFN_HALO:FFN_HALO + tf]
        g0 = ge[FFN_HALO:FFN_HALO + tf]
        gp1 = pltpu.roll(ge, ext - 1, 0)[FFN_HALO:FFN_HALO + tf]
        w = cw_ref[:, lo:lo + FFN_FC]
        gc = (gm1 * w[0:1] + g0 * w[1:2] + gp1 * w[2:3]) + cb_ref[:, lo:lo + FFN_FC]
        h_scr[:, lo:lo + FFN_FC] = (jax.nn.gelu(gc) * val).astype(BF16)
    y = x + _dot(h_scr[...], wdn_ref[...])
    if final_norm:
        y = _rms(y, gf_ref[...])
    o_ref[0] = y


def _conv_ffn(x, gain, w_up, conv_w, conv_b, w_down, final_gain, final_norm, tf=512):
    bsz, s, d = x.shape
    tf = min(tf, s)
    nt = s // tf
    hpt = tf // FFN_HALO
    nhb = s // FFN_HALO
    const = dict(pipeline_mode=pl.Buffered(1))
    return pl.pallas_call(
        functools.partial(_ffn_kernel, tf=tf, final_norm=final_norm),
        out_shape=jax.ShapeDtypeStruct((bsz, s, d), F32),
        grid=(bsz, nt),
        in_specs=[
            pl.BlockSpec((1, FFN_HALO, d), lambda b, i: (b, jnp.maximum(i * hpt - 1, 0), 0)),
            pl.BlockSpec((1, tf, d), lambda b, i: (b, i, 0)),
            pl.BlockSpec((1, FFN_HALO, d), lambda b, i: (b, jnp.minimum((i + 1) * hpt, nhb - 1), 0)),
            pl.BlockSpec((1, d), lambda b, i: (0, 0)),
            pl.BlockSpec((d, 2 * D_FF), lambda b, i: (0, 0), **const),
            pl.BlockSpec((3, D_FF), lambda b, i: (0, 0)),
            pl.BlockSpec((1, D_FF), lambda b, i: (0, 0)),
            pl.BlockSpec((D_FF, d), lambda b, i: (0, 0), **const),
            pl.BlockSpec((1, d), lambda b, i: (0, 0)),
        ],
        out_specs=pl.BlockSpec((1, tf, d), lambda b, i: (b, i, 0)),
        scratch_shapes=[
            pltpu.VMEM((tf + 2 * FFN_HALO, d), BF16),
            pltpu.VMEM((tf, D_FF), BF16),
        ],
        compiler_params=_params("parallel", "parallel"),
        name="conv_ffn",
    )(x, x, x, gain.reshape(1, d), w_up, conv_w, conv_b.reshape(1, D_FF), w_down,
      final_gain.reshape(1, d))


def _prepare(p):
    q = dict(p)
    for name in ("fnet_w_out", "lru_w_in", "lru_w_out", "xa_w_q", "xa_w_kv", "xa_w_o",
                 "ffn_w_up", "ffn_w_down"):
        q[name] = p[name].astype(BF16)
    nblk = D_RNN // LRU_BLOCK
    w_a, w_i = p["lru_w_a"], p["lru_w_i"]
    wg = jnp.concatenate([w_a[:, 0], w_a[:, 1], w_i[:, 0], w_i[:, 1]], axis=-1)
    q["lru_wg"] = (0.5 * wg).astype(BF16)

    def per_block(v):
        v = v.reshape(v.shape[0], 2, nblk, LRU_BLOCK)
        return jnp.concatenate([v[:, 0], v[:, 1]], axis=-1)[:, :, None, :]

    q["lru_bg"] = 0.5 * jnp.concatenate([per_block(p["lru_b_a"]), per_block(p["lru_b_i"])], axis=-1)
    q["lru_nls"] = per_block(0.5 * (-LRU_C * jax.nn.softplus(-p["lru_lambda"].astype(F32))))
    return q


def _trunk(x, mem, p, tables):
    depth = p["norm_mix"].shape[0]
    cw, m = tables
    for i in range(depth):
        j = i // 2
        if i % 2 == 0:
            y = _fnet_mix(x, p["norm_mix"][i], cw, m)
            w_mix, b_mix = p["fnet_w_out"][j], p["fnet_b_out"][j]
        else:
            y = _lru_mix(x, p["norm_mix"][i], p["lru_w_in"][j], p["lru_conv_w"][j], p["lru_conv_b"][j],
                         p["lru_wg"][j], p["lru_bg"][j], p["lru_nls"][j])
            w_mix, b_mix = p["lru_w_out"][j], None
        k, v = _kv_proj(mem, p["norm_mem"][i], p["xa_w_kv"][i])
        x = _mixer_out_cross_attention(x, y, w_mix, b_mix, p["norm_xa"][i], k, v,
                                       p["xa_w_q"][i], p["xa_w_o"][i])
        x = _conv_ffn(x, p["norm_ffn"][i], p["ffn_w_up"][i], p["ffn_conv_w"][i], p["ffn_conv_b"][i],
                      p["ffn_w_down"][i], p["norm_final"], final_norm=(i == depth - 1))
    return x


def kernel(x_prompt, x_sample, mem_prompt, mem_sample, norm_mix, fnet_w_out, fnet_b_out, lru_w_in, lru_conv_w, lru_conv_b, lru_w_a, lru_b_a, lru_w_i, lru_b_i, lru_lambda, lru_w_out, norm_xa, norm_mem, xa_w_q, xa_w_kv, xa_w_o, norm_ffn, ffn_w_up, ffn_conv_w, ffn_conv_b, ffn_w_down, norm_final):
    params = _prepare({
        "norm_mix": norm_mix, "fnet_w_out": fnet_w_out, "fnet_b_out": fnet_b_out,
        "lru_w_in": lru_w_in, "lru_conv_w": lru_conv_w, "lru_conv_b": lru_conv_b,
        "lru_w_a": lru_w_a, "lru_b_a": lru_b_a, "lru_w_i": lru_w_i, "lru_b_i": lru_b_i,
        "lru_lambda": lru_lambda, "lru_w_out": lru_w_out,
        "norm_xa": norm_xa, "norm_mem": norm_mem, "xa_w_q": xa_w_q, "xa_w_kv": xa_w_kv, "xa_w_o": xa_w_o,
        "norm_ffn": norm_ffn, "ffn_w_up": ffn_w_up, "ffn_conv_w": ffn_conv_w, "ffn_conv_b": ffn_conv_b,
        "ffn_w_down": ffn_w_down, "norm_final": norm_final,
    })
    outs = []
    for x, mem in ((x_prompt, mem_prompt), (x_sample, mem_sample)):
        cw, m = _fnet_tables(x.shape[1])
        tables = (jnp.asarray(cw).astype(BF16), jnp.asarray(m).astype(BF16))
        outs.append(_trunk(x, mem, params, tables))
    return tuple(outs)
```

```python
import functools

import numpy as np
import jax
import jax.numpy as jnp
from jax import lax
from jax.experimental import pallas as pl
from jax.experimental.pallas import tpu as pltpu

F32 = jnp.float32
BF16 = jnp.bfloat16

D_MODEL = 1024
EPS = 1e-6
FNET_GROUPS = 4
FNET_W = D_MODEL // FNET_GROUPS
FNET_RADIX = 4
D_RNN = 1280
LRU_BLOCK = 128
LRU_CB = 256
LRU_C = 8.0
LRU_SEGS = 8
LRU_STEPS = 8
F32_TINY = float(np.finfo(np.float32).tiny)
N_MEM = 256
XA_HEADS = 4
XA_HEAD_DIM = D_MODEL // XA_HEADS
D_FF = 2816
FFN_FC = 256
FFN_HALO = 16

LANES = 128
V7X_VMEM_LIMIT_BYTES =56 * 1024 * 1024


def _rms(x, g):
    ms = jnp.mean(x * x, axis=-1, keepdims=True)
    return (x * lax.rsqrt(ms + EPS)) * g


def _dot(a, b):
    return jnp.dot(a, b, preferred_element_type=F32)


def _params(*sem):
    return pltpu.CompilerParams(dimension_semantics=sem, vmem_limit_bytes=V7X_VMEM_LIMIT_BYTES)


def _fnet_kernel(x_ref, g_ref, cw_ref, m_ref, f_ref, xn_scr, u_scr, y_scr, *, seq):
    q = seq // FNET_RADIX
    rows = min(512, seq)
    for r0 in range(0, seq, rows):
        xn_scr[r0:r0 + rows, :] = _rms(x_ref[0, r0:r0 + rows, :], g_ref[...]).astype(BF16)

    for g in range(FNET_GROUPS):
        slot = g % 2
        z = _dot(xn_scr[:, g * FNET_W:(g + 1) * FNET_W], cw_ref[...])
        zr = [z[j * q:(j + 1) * q, :FNET_W] for j in range(4)]
        zi = [z[j * q:(j + 1) * q, FNET_W:] for j in range(4)]
        t0r, t0i = zr[0] + zr[2], zi[0] + zi[2]
        t1r, t1i = zr[0] - zr[2], zi[0] - zi[2]
        t2r, t2i = zr[1] + zr[3], zi[1] + zi[3]
        t3r, t3i = zr[1] - zr[3], zi[1] - zi[3]
        u = [(t0r + t2r, t0i + t2i), (t1r + t3i, t1i - t3r),
             (t0r - t2r, t0i - t2i), (t1r - t3i, t1i + t3r)]
        for r in range(4):
            u_scr[slot, r, :q, :] = u[r][0].astype(BF16)
            u_scr[slot, r, q:, :] = u[r][1].astype(BF16)
        for r in range(4):
            yr = _dot(m_ref[r], u_scr[slot, r])
            for h in range(FNET_W // LANES):
                y_scr.at[slot, h][pl.ds(r, q, stride=FNET_RADIX), :] = yr[:, h * LANES:(h + 1) * LANES]
        for h in range(FNET_W // LANES):
            lo = g * FNET_W + h * LANES
            f_ref[0, :, lo:lo + LANES] = y_scr[slot, h].astype(BF16)


def _fnet_tables(seq):
    w = FNET_W
    cw_idx = np.outer(np.arange(w), np.arange(w)) % w
    ang = 2.0 * np.pi * cw_idx / w
    cw = np.concatenate([np.cos(ang), -np.sin(ang)], axis=1) / np.sqrt(w)
    q = seq // FNET_RADIX
    mats = []
    for r in range(FNET_RADIX):
        kn = np.outer(FNET_RADIX * np.arange(q) + r, np.arange(q)) % seq
        ang = 2.0 * np.pi * kn / seq
        mats.append(np.concatenate([np.cos(ang), np.sin(ang)], axis=1) / np.sqrt(seq))
    m = np.stack(mats)
    return cw.astype(np.float32), m.astype(np.float32)


def _fnet_mix(x, gain, cw, m):
    bsz, s, d = x.shape
    q = s // FNET_RADIX
    return pl.pallas_call(
        functools.partial(_fnet_kernel, seq=s),
        out_shape=jax.ShapeDtypeStruct((bsz, s, d), BF16),
        grid=(bsz,),
        in_specs=[
            pl.BlockSpec((1, s, d), lambda b: (b, 0, 0)),
            pl.BlockSpec((1, d), lambda b: (0, 0)),
            pl.BlockSpec((FNET_W, 2 * FNET_W), lambda b: (0, 0)),
            pl.BlockSpec((FNET_RADIX, q, 2 * q), lambda b: (0, 0, 0), pipeline_mode=pl.Buffered(1)),
        ],
        out_specs=pl.BlockSpec((1, s, d), lambda b: (b, 0, 0)),
        scratch_shapes=[
            pltpu.VMEM((s, d), BF16),
            pltpu.VMEM((2, FNET_RADIX, 2 * q, FNET_W), BF16),
            pltpu.VMEM((2, FNET_W // LANES, s, LANES), F32),
        ],
        compiler_params=_params("parallel"),
        name="fnet_dft",
    )(x, gain.reshape(1, d), cw, m)


def _lru_kernel(x_ref, g_ref, wgate_ref, wrec_ref, cw_ref, cb_ref, wg_ref, bg_ref, nls_ref,
                y_ref, xn_scr, rec_scr, ch_scr, c16_scr, gts_scr, hp_scr, yp_scr, *, seq):
    ci = pl.program_id(1)
    seg = seq // LRU_SEGS
    cb_w = LRU_CB
    nsub = cb_w // LRU_BLOCK
    rc = min(512, seq)

    def p_tile(d, sub):
        return hp_scr.at[d * nsub + sub]

    def h_tile(d, sub):
        return hp_scr.at[2 * nsub + d * nsub + sub]

    @pl.when(ci == 0)
    def _():
        for i in range(LRU_SEGS):
            xn = _rms(x_ref[0, i * seg:(i + 1) * seg, :], g_ref[...])
            for t in range(D_MODEL // LANES):
                hp_scr.at[t][pl.ds(i, seg, stride=LRU_SEGS), :] = xn[:, t * LANES:(t + 1) * LANES]
        for t in range(D_MODEL // LANES):
            xn_scr[:, t * LANES:(t + 1) * LANES] = hp_scr[t].astype(BF16)

    rec_scr[...] = _dot(xn_scr[...], wrec_ref[...])

    def edge_down(blk):
        row = lax.broadcasted_iota(jnp.int32, blk.shape, 0)
        return jnp.where(row == 0, 0.0, pltpu.roll(blk, 1, 0))

    def edge_up(blk):
        row = lax.broadcasted_iota(jnp.int32, blk.shape, 0)
        return jnp.where(row == LRU_SEGS - 1, 0.0, pltpu.roll(blk, LRU_SEGS - 1, 0))

    def shifted(k, r0):
        if k > 0:
            lo = r0 - 8 * k
            if lo >= 0:
                return rec_scr[lo:lo + rc, :]
            heads = [edge_down(rec_scr[seq - 8 * (k - e):seq - 8 * (k - e) + 8, :]) for e in range(k)]
            return jnp.concatenate(heads + [rec_scr[0:rc - 8 * k, :]], axis=0)
        hi = r0 + 8
        if hi + rc <= seq:
            return rec_scr[hi:hi + rc, :]
        return jnp.concatenate([rec_scr[hi:seq, :], edge_up(rec_scr[0:8, :])], axis=0)

    cw = cw_ref[...]
    for r0 in range(0, seq, rc):
        c = (shifted(2, r0) * cw[0:1] + shifted(1, r0) * cw[1:2]
             + rec_scr[r0:r0 + rc, :] * cw[2:3] + shifted(-1, r0) * cw[3:4]) + cb_ref[...]
        ch_scr[r0:r0 + rc, :] = 0.5 * c
        c16_scr[r0:r0 + rc, :] = c.astype(BF16)

    def gate_matmuls(q):
        rows = slice(q * rc, (q + 1) * rc)
        for sub in range(nsub):
            gts_scr[sub, rows, :] = _dot(c16_scr[rows, sub * LRU_BLOCK:(sub + 1) * LRU_BLOCK], wg_ref[sub])

    def output_gate(q):
        rows = slice(q * rc, (q + 1) * rc)
        rec_scr[rows, :] = jax.nn.gelu(_dot(xn_scr[rows, :], wgate_ref[...]))

    chains = [(d, sub) for d in range(2) for sub in range(nsub)]
    blk_rows = 8 * LRU_STEPS
    nblk = seg // LRU_STEPS

    def scan_block(it, carry):
        new = []
        for (d, sub), (h, p) in zip(chains, carry):
            row0 = (it if d == 0 else nblk - 1 - it) * blk_rows
            rows = slice(row0, row0 + blk_rows)
            ta = jnp.tanh(gts_scr[sub, rows, d * LANES:(d + 1) * LANES]
                          + bg_ref[sub, :, d * LANES:(d + 1) * LANES])
            ti = jnp.tanh(gts_scr[sub, rows, (2 + d) * LANES:(3 + d) * LANES]
                          + bg_ref[sub, :, (2 + d) * LANES:(3 + d) * LANES])
            nlsh = nls_ref[sub, :, d * LANES:(d + 1) * LANES]
            log_a = nlsh * ta + nlsh
            a = jnp.exp(log_a)
            z = jnp.tanh(log_a) * (-1.0 - a * a)
            ch = ch_scr[rows, sub * LANES:(sub + 1) * LANES]
            b = (z * lax.rsqrt(jnp.maximum(z, F32_TINY))) * (ch * ti + ch)
            hs, ps = [None] * LRU_STEPS, [None] * LRU_STEPS
            for k in (range(LRU_STEPS) if d == 0 else reversed(range(LRU_STEPS))):
                ak = a[8 * k:8 * k + 8]
                h = ak * h + b[8 * k:8 * k + 8]
                p = ak * p
                hs[k], ps[k] = h, p
            h_tile(d, sub)[rows, :] = jnp.concatenate(hs, axis=0)
            p_tile(d, sub)[rows, :] = jnp.concatenate(ps, axis=0)
            new.append((h, p))
        return tuple(new)

    zeros = jnp.zeros((8, LANES), F32)
    ones = jnp.ones((8, LANES), F32)
    nchunk = seq // rc
    bpc = nblk // nchunk
    ready = set()
    state = tuple((zeros, ones) for _ in chains)
    for q in range(nchunk):
        for need in (q, nchunk - 1 - q):
            if need not in ready:
                gate_matmuls(need)
                ready.add(need)
        output_gate(q)
        for it in range(q * bpc, (q + 1) * bpc):
            state = scan_block(it, state)
    ends = state

    carries = {}
    for (d, sub), (h_end, p_end) in zip(chains, ends):
        move = edge_down if d == 0 else edge_up
        e, q = move(h_end), move(p_end)
        cin = jnp.zeros((8, LANES), F32)
        for _ in range(LRU_SEGS - 1):
            cin = e + q * move(cin)
        carries[(d, sub)] = cin

    for r0 in range(0, seq, rc):
        reps = rc // 8
        gate = rec_scr[r0:r0 + rc, :]
        for sub in range(nsub):
            hf = (h_tile(0, sub)[r0:r0 + rc, :]
                  + p_tile(0, sub)[r0:r0 + rc, :] * jnp.tile(carries[(0, sub)], (reps, 1)))
            hb = (h_tile(1, sub)[r0:r0 + rc, :]
                  + p_tile(1, sub)[r0:r0 + rc, :] * jnp.tile(carries[(1, sub)], (reps, 1)))
            yp_scr.at[sub][r0:r0 + rc, :] = (hf + hb) * gate[:, sub * LANES:(sub + 1) * LANES]

    for i in range(LRU_SEGS):
        for sub in range(nsub):
            y_ref[0, i * seg:(i + 1) * seg, sub * LANES:(sub + 1) * LANES] = (
                yp_scr.at[sub][pl.ds(i, seg, stride=LRU_SEGS), :].astype(BF16))


def _lru_mix(x, gain, w_in, conv_w, conv_b, wg, bg, nls):
    bsz, s, d = x.shape
    nb = D_RNN // LRU_CB
    sub = LRU_CB // LRU_BLOCK
    assert d // LANES >= 4 * sub and LRU_BLOCK == LANES
    return pl.pallas_call(
        functools.partial(_lru_kernel, seq=s),
        out_shape=jax.ShapeDtypeStruct((bsz, s, D_RNN), BF16),
        grid=(bsz, nb),
        in_specs=[
            pl.BlockSpec((1, s, d), lambda b, c: (b, 0, 0)),
            pl.BlockSpec((1, d), lambda b, c: (0, 0)),
            pl.BlockSpec((d, LRU_CB), lambda b, c: (0, c)),
            pl.BlockSpec((d, LRU_CB), lambda b, c: (0, nb + c)),
            pl.BlockSpec((4, LRU_CB), lambda b, c: (0, c)),
            pl.BlockSpec((1, LRU_CB), lambda b, c: (0, c)),
            pl.BlockSpec((sub, LRU_BLOCK, 4 * LRU_BLOCK), lambda b, c: (c, 0, 0)),
            pl.BlockSpec((sub, 1, 4 * LRU_BLOCK), lambda b, c: (c, 0, 0)),
            pl.BlockSpec((sub, 1, 2 * LRU_BLOCK), lambda b, c: (c, 0, 0)),
        ],
        out_specs=pl.BlockSpec((1, s, LRU_CB), lambda b, c: (b, 0, c)),
        scratch_shapes=[
            pltpu.VMEM((s, d), BF16),
            pltpu.VMEM((s, LRU_CB), F32),
            pltpu.VMEM((s, LRU_CB), F32),
            pltpu.VMEM((s, LRU_CB), BF16),
            pltpu.VMEM((sub, s, 4 * LRU_BLOCK), F32),
            pltpu.VMEM((d // LANES, s, LANES), F32),
            pltpu.VMEM((sub, s, LANES), F32),
        ],
        compiler_params=_params("parallel", "arbitrary"),
        name="lru_core",
    )(x, gain.reshape(1, d), w_in, w_in, conv_w, conv_b.reshape(1, D_RNN), wg, bg, nls)


def _kv_kernel(m_ref, g_ref, w_ref, k_ref, v_ref):
    mn = _rms(m_ref[...], g_ref[...]).astype(BF16)
    kv = _dot(mn, w_ref[...])
    k_ref[...] = kv[:, :D_MODEL].astype(BF16)
    v_ref[...] = kv[:, D_MODEL:].astype(BF16)


def _kv_proj(mem, gain, w_kv, tm=512):
    bsz, m, d = mem.shape
    rows = bsz * m
    tm = min(tm, rows)
    k, v = pl.pallas_call(
        _kv_kernel,
        out_shape=(jax.ShapeDtypeStruct((rows, d), BF16), jax.ShapeDtypeStruct((rows, d), BF16)),
        grid=(rows // tm,),
        in_specs=[
            pl.BlockSpec((tm, d), lambda i: (i, 0)),
            pl.BlockSpec((1, d), lambda i: (0, 0)),
            pl.BlockSpec((d, 2 * d), lambda i: (0, 0)),
        ],
        out_specs=(pl.BlockSpec((tm, d), lambda i: (i, 0)), pl.BlockSpec((tm, d), lambda i: (i, 0))),
        compiler_params=_params("parallel"),
        name="kv_proj",
    )(mem.reshape(rows, d), gain.reshape(1, d), w_kv)
    return k.reshape(bsz, m, d), v.reshape(bsz, m, d)


def _xattn_kernel(x_ref, y_ref, wm_ref, bm_ref, g_ref, k_ref, v_ref, wq_ref, wo_ref, o_ref, oh_scr,
                  *, mixer_bias):
    mix = _dot(y_ref[0], wm_ref[...])
    if mixer_bias:
        mix = mix + bm_ref[...]
    x = x_ref[0] + mix
    xn = _rms(x, g_ref[...]).astype(BF16)
    q = _dot(xn, wq_ref[...])
    for h in range(XA_HEADS):
        lo = h * XA_HEAD_DIM
        qh = q[:, lo:lo + XA_HEAD_DIM].astype(BF16)
        kh = k_ref[0, :, lo:lo + XA_HEAD_DIM]
        s = lax.dot_general(qh, kh, (((1,), (1,)), ((), ())), preferred_element_type=F32)
        s = s * (XA_HEAD_DIM ** -0.5)
        e = jnp.exp(s - jnp.max(s, axis=-1, keepdims=True))
        p = e / jnp.sum(e, axis=-1, keepdims=True)
        oh = _dot(p.astype(BF16), v_ref[0, :, lo:lo + XA_HEAD_DIM])
        oh_scr[:, lo:lo + XA_HEAD_DIM] = oh.astype(BF16)
    o_ref[0] = x + _dot(oh_scr[...], wo_ref[...])


def _mixer_out_cross_attention(x, y, w_mix, b_mix, gain, k, v, w_q, w_o, tq=1024):
    bsz, s, d = x.shape
    m = k.shape[1]
    ky = y.shape[-1]
    tq = min(tq, s)
    mixer_bias = b_mix is not None
    if not mixer_bias:
        b_mix = jnp.zeros((d,), F32)
    return pl.pallas_call(
        functools.partial(_xattn_kernel, mixer_bias=mixer_bias),
        out_shape=jax.ShapeDtypeStruct((bsz, s, d), F32),
        grid=(bsz, s // tq),
        in_specs=[
            pl.BlockSpec((1, tq, d), lambda b, i: (b, i, 0)),
            pl.BlockSpec((1, tq, ky), lambda b, i: (b, i, 0)),
            pl.BlockSpec((ky, d), lambda b, i: (0, 0)),
            pl.BlockSpec((1, d), lambda b, i: (0, 0)),
            pl.BlockSpec((1, d), lambda b, i: (0, 0)),
            pl.BlockSpec((1, m, d), lambda b, i: (b, 0, 0)),
            pl.BlockSpec((1, m, d), lambda b, i: (b, 0, 0)),
            pl.BlockSpec((d, d), lambda b, i: (0, 0)),
            pl.BlockSpec((d, d), lambda b, i: (0, 0)),
        ],
        out_specs=pl.BlockSpec((1, tq, d), lambda b, i: (b, i, 0)),
        scratch_shapes=[pltpu.VMEM((tq, d), BF16)],
        compiler_params=_params("parallel", "parallel"),
        name="cross_attention",
    )(x, y, w_mix, b_mix.reshape(1, d), gain.reshape(1, d), k, v, w_q, w_o)


def _ffn_kernel(xp_ref, x_ref, xq_ref, g_ref, wup_ref, cw_ref, cb_ref, wdn_ref, gf_ref, o_ref,
                xe_scr, h_scr, *, tf, final_norm):
    i = pl.program_id(1)
    last = pl.num_programs(1) - 1
    g = g_ref[...]
    ext = tf + 2 * FFN_HALO
    x = x_ref[0]
    xe_scr[0:FFN_HALO, :] = jnp.where(i > 0, _rms(xp_ref[0], g), 0.0).astype(BF16)
    xe_scr[FFN_HALO:FFN_HALO + tf, :] = _rms(x, g).astype(BF16)
    xe_scr[FFN_HALO + tf:ext, :] = jnp.where(i < last, _rms(xq_ref[0], g), 0.0).astype(BF16)
    for c in range(D_FF // FFN_FC):
        lo = c * FFN_FC
        ge = _dot(xe_scr[...], wup_ref[:, lo:lo + FFN_FC])
        val = _dot(xe_scr[FFN_HALO:FFN_HALO + tf, :], wup_ref[:, D_FF + lo:D_FF + lo + FFN_FC])
        gm1 = pltpu.roll(ge, 1, 0)[FFN_HALO:FFN_HALO + tf]
        g0 = ge[FFN_HALO:FFN_HALO + tf]
        gp1 = pltpu.roll(ge, ext - 1, 0)[FFN_HALO:FFN_HALO + tf]
        w = cw_ref[:, lo:lo + FFN_FC]
        gc = (gm1 * w[0:1] + g0 * w[1:2] + gp1 * w[2:3]) + cb_ref[:, lo:lo + FFN_FC]
        h_scr[:, lo:lo + FFN_FC] = (jax.nn.gelu(gc) * val).astype(BF16)
    y = x + _dot(h_scr[...], wdn_ref[...])
    if final_norm:
        y = _rms(y, gf_ref[...])
    o_ref[0] = y


def _conv_ffn(x, gain, w_up, conv_w, conv_b, w_down, final_gain, final_norm, tf=1024):
    bsz, s, d = x.shape
    tf = min(tf, s)
    nt = s // tf
    hpt = tf // FFN_HALO
    nhb = s // FFN_HALO
    const = dict(pipeline_mode=pl.Buffered(1))
    return pl.pallas_call(
        functools.partial(_ffn_kernel, tf=tf, final_norm=final_norm),
        out_shape=jax.ShapeDtypeStruct((bsz, s, d), F32),
        grid=(bsz, nt),
        in_specs=[
            pl.BlockSpec((1, FFN_HALO, d), lambda b, i: (b, jnp.maximum(i * hpt - 1, 0), 0)),
            pl.BlockSpec((1, tf, d), lambda b, i: (b, i, 0)),
            pl.BlockSpec((1, FFN_HALO, d), lambda b, i: (b, jnp.minimum((i + 1) * hpt, nhb - 1), 0)),
            pl.BlockSpec((1, d), lambda b, i: (0, 0)),
            pl.BlockSpec((d, 2 * D_FF), lambda b, i: (0, 0), **const),
            pl.BlockSpec((3, D_FF), lambda b, i: (0, 0)),
            pl.BlockSpec((1, D_FF), lambda b, i: (0, 0)),
            pl.BlockSpec((D_FF, d), lambda b, i: (0, 0), **const),
            pl.BlockSpec((1, d), lambda b, i: (0, 0)),
        ],
        out_specs=pl.BlockSpec((1, tf, d), lambda b, i: (b, i, 0)),
        scratch_shapes=[
            pltpu.VMEM((tf + 2 * FFN_HALO, d), BF16),
            pltpu.VMEM((tf, D_FF), BF16),
        ],
        compiler_params=_params("parallel", "parallel"),
        name="conv_ffn",
    )(x, x, x, gain.reshape(1, d), w_up, conv_w, conv_b.reshape(1, D_FF), w_down,
      final_gain.reshape(1, d))


def _prepare(p):
    q = dict(p)
    for name in ("fnet_w_out", "lru_w_in", "lru_w_out", "xa_w_q", "xa_w_kv", "xa_w_o",
                 "ffn_w_up", "ffn_w_down"):
        q[name] = p[name].astype(BF16)
    nblk = D_RNN // LRU_BLOCK
    w_a, w_i = p["lru_w_a"], p["lru_w_i"]
    wg = jnp.concatenate([w_a[:, 0], w_a[:, 1], w_i[:, 0], w_i[:, 1]], axis=-1)
    q["lru_wg"] = (0.5 * wg).astype(BF16)

    def per_block(v):
        v = v.reshape(v.shape[0], 2, nblk, LRU_BLOCK)
        return jnp.concatenate([v[:, 0], v[:, 1]], axis=-1)[:, :, None, :]

    q["lru_bg"] = 0.5 * jnp.concatenate([per_block(p["lru_b_a"]), per_block(p["lru_b_i"])], axis=-1)
    q["lru_nls"] = per_block(0.5 * (-LRU_C * jax.nn.softplus(-p["lru_lambda"].astype(F32))))
    return q


def _trunk(x, mem, p, tables):
    depth = p["norm_mix"].shape[0]
    cw, m = tables
    for i in range(depth):
        j = i // 2
        if i % 2 == 0:
            y = _fnet_mix(x, p["norm_mix"][i], cw, m)
            w_mix, b_mix = p["fnet_w_out"][j], p["fnet_b_out"][j]
        else:
            y = _lru_mix(x, p["norm_mix"][i], p["lru_w_in"][j], p["lru_conv_w"][j], p["lru_conv_b"][j],
                         p["lru_wg"][j], p["lru_bg"][j], p["lru_nls"][j])
            w_mix, b_mix = p["lru_w_out"][j], None
        k, v = _kv_proj(mem, p["norm_mem"][i], p["xa_w_kv"][i])
        x = _mixer_out_cross_attention(x, y, w_mix, b_mix, p["norm_xa"][i], k, v,
                                       p["xa_w_q"][i], p["xa_w_o"][i])
        x = _conv_ffn(x, p["norm_ffn"][i], p["ffn_w_up"][i], p["ffn_conv_w"][i], p["ffn_conv_b"][i],
                      p["ffn_w_down"][i], p["norm_final"], final_norm=(i == depth - 1))
    return x


def kernel(x_prompt, x_sample, mem_prompt, mem_sample, norm_mix, fnet_w_out, fnet_b_out, lru_w_in, lru_conv_w, lru_conv_b, lru_w_a, lru_b_a, lru_w_i, lru_b_i, lru_lambda, lru_w_out, norm_xa, norm_mem, xa_w_q, xa_w_kv, xa_w_o, norm_ffn, ffn_w_up, ffn_conv_w, ffn_conv_b, ffn_w_down, norm_final):
    params = _prepare({
        "norm_mix": norm_mix, "fnet_w_out": fnet_w_out, "fnet_b_out": fnet_b_out,
        "lru_w_in": lru_w_in, "lru_conv_w": lru_conv_w, "lru_conv_b": lru_conv_b,
        "lru_w_a": lru_w_a, "lru_b_a": lru_b_a, "lru_w_i": lru_w_i, "lru_b_i": lru_b_i,
        "lru_lambda": lru_lambda, "lru_w_out": lru_w_out,
        "norm_xa": norm_xa, "norm_mem": norm_mem, "xa_w_q": xa_w_q, "xa_w_kv": xa_w_kv, "xa_w_o": xa_w_o,
        "norm_ffn": norm_ffn, "ffn_w_up": ffn_w_up, "ffn_conv_w": ffn_conv_w, "ffn_conv_b": ffn_conv_b,
        "ffn_w_down": ffn_w_down, "norm_final": norm_final,
    })
    outs = []
    for x, mem in ((x_prompt, mem_prompt), (x_sample, mem_sample)):
        cw, m = _fnet_tables(x.shape[1])
        tables = (jnp.asarray(cw).astype(BF16), jnp.asarray(m).astype(BF16))
        outs.append(_trunk(x, mem, params, tables))
    return tuple(outs)
```

```python
import functools

import numpy as np
import jax
import jax.numpy as jnp
from jax import lax
from jax.experimental import pallas as pl
from jax.experimental.pallas import tpu as pltpu

F32 = jnp.float32
BF16 = jnp.bfloat16

D_MODEL = 1024
EPS = 1e-6
FNET_GROUPS = 4
FNET_W = D_MODEL // FNET_GROUPS
FNET_RADIX = 4
D_RNN = 1280
LRU_BLOCK = 128
LRU_CB = 256
LRU_C = 8.0
LRU_SEGS = 8
LRU_STEPS = 8
LRU_CHUNK_ROWS = 256
F32_TINY = float(np.finfo(np.float32).tiny)
N_MEM = 256
XA_HEADS = 4
XA_HEAD_DIM = D_MODEL // XA_HEADS
D_FF = 2816
FFN_FC = 256
FFN_HALO = 16

LANES = 128
V7X_VMEM_LIMIT_BYTES =56 * 1024 * 1024


def _rms(x, g):
    ms = jnp.mean(x * x, axis=-1, keepdims=True)
    return (x * lax.rsqrt(ms + EPS)) * g


def _dot(a, b):
    return jnp.dot(a, b, preferred_element_type=F32)


def _params(*sem):
    return pltpu.CompilerParams(dimension_semantics=sem, vmem_limit_bytes=V7X_VMEM_LIMIT_BYTES)


def _fnet_kernel(x_ref, g_ref, cw_ref, m_ref, f_ref, xn_scr, u_scr, y_scr, *, seq):
    q = seq // FNET_RADIX
    rows = min(512, seq)
    for r0 in range(0, seq, rows):
        xn_scr[r0:r0 + rows, :] = _rms(x_ref[0, r0:r0 + rows, :], g_ref[...]).astype(BF16)

    for g in range(FNET_GROUPS):
        slot = g % 2
        z = _dot(xn_scr[:, g * FNET_W:(g + 1) * FNET_W], cw_ref[...])
        zr = [z[j * q:(j + 1) * q, :FNET_W] for j in range(4)]
        zi = [z[j * q:(j + 1) * q, FNET_W:] for j in range(4)]
        t0r, t0i = zr[0] + zr[2], zi[0] + zi[2]
        t1r, t1i = zr[0] - zr[2], zi[0] - zi[2]
        t2r, t2i = zr[1] + zr[3], zi[1] + zi[3]
        t3r, t3i = zr[1] - zr[3], zi[1] - zi[3]
        u = [(t0r + t2r, t0i + t2i), (t1r + t3i, t1i - t3r),
             (t0r - t2r, t0i - t2i), (t1r - t3i, t1i + t3r)]
        for r in range(4):
            u_scr[slot, r, :q, :] = u[r][0].astype(BF16)
            u_scr[slot, r, q:, :] = u[r][1].astype(BF16)
        for r in range(4):
            yr = _dot(m_ref[r], u_scr[slot, r])
            for h in range(FNET_W // LANES):
                y_scr.at[slot, h][pl.ds(r, q, stride=FNET_RADIX), :] = yr[:, h * LANES:(h + 1) * LANES]
        for h in range(FNET_W // LANES):
            lo = g * FNET_W + h * LANES
            f_ref[0, :, lo:lo + LANES] = y_scr[slot, h].astype(BF16)


def _fnet_tables(seq):
    w = FNET_W
    cw_idx = np.outer(np.arange(w), np.arange(w)) % w
    ang = 2.0 * np.pi * cw_idx / w
    cw = np.concatenate([np.cos(ang), -np.sin(ang)], axis=1) / np.sqrt(w)
    q = seq // FNET_RADIX
    mats = []
    for r in range(FNET_RADIX):
        kn = np.outer(FNET_RADIX * np.arange(q) + r, np.arange(q)) % seq
        ang = 2.0 * np.pi * kn / seq
        mats.append(np.concatenate([np.cos(ang), np.sin(ang)], axis=1) / np.sqrt(seq))
    m = np.stack(mats)
    return cw.astype(np.float32), m.astype(np.float32)


def _fnet_mix(x, gain, cw, m):
    bsz, s, d = x.shape
    q = s // FNET_RADIX
    return pl.pallas_call(
        functools.partial(_fnet_kernel, seq=s),
        out_shape=jax.ShapeDtypeStruct((bsz, s, d), BF16),
        grid=(bsz,),
        in_specs=[
            pl.BlockSpec((1, s, d), lambda b: (b, 0, 0)),
            pl.BlockSpec((1, d), lambda b: (0, 0)),
            pl.BlockSpec((FNET_W, 2 * FNET_W), lambda b: (0, 0)),
            pl.BlockSpec((FNET_RADIX, q, 2 * q), lambda b: (0, 0, 0), pipeline_mode=pl.Buffered(1)),
        ],
        out_specs=pl.BlockSpec((1, s, d), lambda b: (b, 0, 0)),
        scratch_shapes=[
            pltpu.VMEM((s, d), BF16),
            pltpu.VMEM((2, FNET_RADIX, 2 * q, FNET_W), BF16),
            pltpu.VMEM((2, FNET_W // LANES, s, LANES), F32),
        ],
        compiler_params=_params("parallel"),
        name="fnet_dft",
    )(x, gain.reshape(1, d), cw, m)


def _lru_kernel(x_ref, g_ref, wgate_ref, wrec_ref, cw_ref, cb_ref, wg_ref, bg_ref, nls_ref,
                y_ref, xn_scr, rec_scr, ch_scr, c16_scr, gts_scr, hp_scr, yp_scr, *, seq):
    ci = pl.program_id(1)
    seg = seq // LRU_SEGS
    cb_w = LRU_CB
    nsub = cb_w // LRU_BLOCK
    rc = min(LRU_CHUNK_ROWS, seq)

    def p_tile(d, sub):
        return hp_scr.at[d * nsub + sub]

    def h_tile(d, sub):
        return hp_scr.at[2 * nsub + d * nsub + sub]

    @pl.when(ci == 0)
    def _():
        for i in range(LRU_SEGS):
            xn = _rms(x_ref[0, i * seg:(i + 1) * seg, :], g_ref[...])
            for t in range(D_MODEL // LANES):
                hp_scr.at[t][pl.ds(i, seg, stride=LRU_SEGS), :] = xn[:, t * LANES:(t + 1) * LANES]
        for t in range(D_MODEL // LANES):
            xn_scr[:, t * LANES:(t + 1) * LANES] = hp_scr[t].astype(BF16)

    for r0 in range(0, seq, rc):
        rec_scr[r0:r0 + rc, :] = _dot(xn_scr[r0:r0 + rc, :], wrec_ref[...])

    def edge_down(blk, s=1):
        row = lax.broadcasted_iota(jnp.int32, blk.shape, 0)
        return jnp.where(row < s, 0.0, pltpu.roll(blk, s, 0))

    def edge_up(blk, s=1):
        row = lax.broadcasted_iota(jnp.int32, blk.shape, 0)
        return jnp.where(row >= LRU_SEGS - s, 0.0, pltpu.roll(blk, LRU_SEGS - s, 0))

    def shifted(k, r0):
        if k > 0:
            lo = r0 - 8 * k
            if lo >= 0:
                return rec_scr[lo:lo + rc, :]
            heads = [edge_down(rec_scr[seq - 8 * (k - e):seq - 8 * (k - e) + 8, :]) for e in range(k)]
            return jnp.concatenate(heads + [rec_scr[0:rc - 8 * k, :]], axis=0)
        hi = r0 + 8
        if hi + rc <= seq:
            return rec_scr[hi:hi + rc, :]
        return jnp.concatenate([rec_scr[hi:seq, :], edge_up(rec_scr[0:8, :])], axis=0)

    cw = cw_ref[...]
    for r0 in range(0, seq, rc):
        c = (shifted(2, r0) * cw[0:1] + shifted(1, r0) * cw[1:2]
             + rec_scr[r0:r0 + rc, :] * cw[2:3] + shifted(-1, r0) * cw[3:4]) + cb_ref[...]
        ch_scr[r0:r0 + rc, :] = 0.5 * c
        c16_scr[r0:r0 + rc, :] = c.astype(BF16)

    def gate_matmuls(q):
        rows = slice(q * rc, (q + 1) * rc)
        for sub in range(nsub):
            gts_scr[sub, rows, :] = _dot(c16_scr[rows, sub * LRU_BLOCK:(sub + 1) * LRU_BLOCK], wg_ref[sub])

    def output_gate(q):
        rows = slice(q * rc, (q + 1) * rc)
        rec_scr[rows, :] = jax.nn.gelu(_dot(xn_scr[rows, :], wgate_ref[...]))

    chains = [(d, sub) for d in range(2) for sub in range(nsub)]
    blk_rows = 8 * LRU_STEPS
    nblk = seg // LRU_STEPS

    def scan_block(it, carry):
        new = []
        for (d, sub), (h, p) in zip(chains, carry):
            row0 = (it if d == 0 else nblk - 1 - it) * blk_rows
            rows = slice(row0, row0 + blk_rows)
            ta = jnp.tanh(gts_scr[sub, rows, d * LANES:(d + 1) * LANES]
                          + bg_ref[sub, :, d * LANES:(d + 1) * LANES])
            ti = jnp.tanh(gts_scr[sub, rows, (2 + d) * LANES:(3 + d) * LANES]
                          + bg_ref[sub, :, (2 + d) * LANES:(3 + d) * LANES])
            nlsh = nls_ref[sub, :, d * LANES:(d + 1) * LANES]
            log_a = nlsh * ta + nlsh
            a = jnp.exp(log_a)
            z = jnp.tanh(log_a) * (-1.0 - a * a)
            ch = ch_scr[rows, sub * LANES:(sub + 1) * LANES]
            b = (z * lax.rsqrt(jnp.maximum(z, F32_TINY))) * (ch * ti + ch)
            hs, ps = [None] * LRU_STEPS, [None] * LRU_STEPS
            for k in (range(LRU_STEPS) if d == 0 else reversed(range(LRU_STEPS))):
                ak = a[8 * k:8 * k + 8]
                h = ak * h + b[8 * k:8 * k + 8]
                p = ak * p
                hs[k], ps[k] = h, p
            h_tile(d, sub)[rows, :] = jnp.concatenate(hs, axis=0)
            p_tile(d, sub)[rows, :] = jnp.concatenate(ps, axis=0)
            new.append((h, p))
        return tuple(new)

    zeros = jnp.zeros((8, LANES), F32)
    ones = jnp.ones((8, LANES), F32)
    nchunk = seq // rc
    bpc = nblk // nchunk
    ready = set()
    state = tuple((zeros, ones) for _ in chains)
    for q in range(nchunk):
        for need in (q, nchunk - 1 - q):
            if need not in ready:
                gate_matmuls(need)
                ready.add(need)
        output_gate(q)
        for it in range(q * bpc, (q + 1) * bpc):
            state = scan_block(it, state)
    ends = state

    carries = {}
    for (d, sub), (h_end, p_end) in zip(chains, ends):
        move = edge_down if d == 0 else edge_up
        e, q = move(h_end), move(p_end)
        for s in (1, 2, 4):
            e, q = e + q * move(e, s), q * move(q, s)
        carries[(d, sub)] = e

    for r0 in range(0, seq, rc):
        reps = rc // 8
        gate = rec_scr[r0:r0 + rc, :]
        for sub in range(nsub):
            hf = (h_tile(0, sub)[r0:r0 + rc, :]
                  + p_tile(0, sub)[r0:r0 + rc, :] * jnp.tile(carries[(0, sub)], (reps, 1)))
            hb = (h_tile(1, sub)[r0:r0 + rc, :]
                  + p_tile(1, sub)[r0:r0 + rc, :] * jnp.tile(carries[(1, sub)], (reps, 1)))
            yp_scr.at[sub][r0:r0 + rc, :] = (hf + hb) * gate[:, sub * LANES:(sub + 1) * LANES]

    for i in range(LRU_SEGS):
        for sub in range(nsub):
            y_ref[0, i * seg:(i + 1) * seg, sub * LANES:(sub + 1) * LANES] = (
                yp_scr.at[sub][pl.ds(i, seg, stride=LRU_SEGS), :].astype(BF16))


def _lru_mix(x, gain, w_in, conv_w, conv_b, wg, bg, nls):
    bsz, s, d = x.shape
    nb = D_RNN // LRU_CB
    sub = LRU_CB // LRU_BLOCK
    assert d // LANES >= 4 * sub and LRU_BLOCK == LANES
    return pl.pallas_call(
        functools.partial(_lru_kernel, seq=s),
        out_shape=jax.ShapeDtypeStruct((bsz, s, D_RNN), BF16),
        grid=(bsz, nb),
        in_specs=[
            pl.BlockSpec((1, s, d), lambda b, c: (b, 0, 0)),
            pl.BlockSpec((1, d), lambda b, c: (0, 0)),
            pl.BlockSpec((d, LRU_CB), lambda b, c: (0, c)),
            pl.BlockSpec((d, LRU_CB), lambda b, c: (0, nb + c)),
            pl.BlockSpec((4, LRU_CB), lambda b, c: (0, c)),
            pl.BlockSpec((1, LRU_CB), lambda b, c: (0, c)),
            pl.BlockSpec((sub, LRU_BLOCK, 4 * LRU_BLOCK), lambda b, c: (c, 0, 0)),
            pl.BlockSpec((sub, 1, 4 * LRU_BLOCK), lambda b, c: (c, 0, 0)),
            pl.BlockSpec((sub, 1, 2 * LRU_BLOCK), lambda b, c: (c, 0, 0)),
        ],
        out_specs=pl.BlockSpec((1, s, LRU_CB), lambda b, c: (b, 0, c)),
        scratch_shapes=[
            pltpu.VMEM((s, d), BF16),
            pltpu.VMEM((s, LRU_CB), F32),
            pltpu.VMEM((s, LRU_CB), F32),
            pltpu.VMEM((s, LRU_CB), BF16),
            pltpu.VMEM((sub, s, 4 * LRU_BLOCK), F32),
            pltpu.VMEM((d // LANES, s, LANES), F32),
            pltpu.VMEM((sub, s, LANES), F32),
        ],
        compiler_params=_params("parallel", "arbitrary"),
        name="lru_core",
    )(x, gain.reshape(1, d), w_in, w_in, conv_w, conv_b.reshape(1, D_RNN), wg, bg, nls)


def _kv_kernel(m_ref, g_ref, w_ref, k_ref, v_ref):
    mn = _rms(m_ref[...], g_ref[...]).astype(BF16)
    kv = _dot(mn, w_ref[...])
    k_ref[...] = kv[:, :D_MODEL].astype(BF16)
    v_ref[...] = kv[:, D_MODEL:].astype(BF16)


def _kv_proj(mem, gain, w_kv, tm=512):
    bsz, m, d = mem.shape
    rows = bsz * m
    tm = min(tm, rows)
    k, v = pl.pallas_call(
        _kv_kernel,
        out_shape=(jax.ShapeDtypeStruct((rows, d), BF16), jax.ShapeDtypeStruct((rows, d), BF16)),
        grid=(rows // tm,),
        in_specs=[
            pl.BlockSpec((tm, d), lambda i: (i, 0)),
            pl.BlockSpec((1, d), lambda i: (0, 0)),
            pl.BlockSpec((d, 2 * d), lambda i: (0, 0)),
        ],
        out_specs=(pl.BlockSpec((tm, d), lambda i: (i, 0)), pl.BlockSpec((tm, d), lambda i: (i, 0))),
        compiler_params=_params("parallel"),
        name="kv_proj",
    )(mem.reshape(rows, d), gain.reshape(1, d), w_kv)
    return k.reshape(bsz, m, d), v.reshape(bsz, m, d)


def _xattn_kernel(x_ref, y_ref, wm_ref, bm_ref, g_ref, k_ref, v_ref, wq_ref, wo_ref, o_ref, oh_scr,
                  *, mixer_bias):
    mix = _dot(y_ref[0], wm_ref[...])
    if mixer_bias:
        mix = mix + bm_ref[...]
    x = x_ref[0] + mix
    xn = _rms(x, g_ref[...]).astype(BF16)
    q = _dot(xn, wq_ref[...])
    for h in range(XA_HEADS):
        lo = h * XA_HEAD_DIM
        qh = q[:, lo:lo + XA_HEAD_DIM].astype(BF16)
        kh = k_ref[0, :, lo:lo + XA_HEAD_DIM]
        s = lax.dot_general(qh, kh, (((1,), (1,)), ((), ())), preferred_element_type=F32)
        s = s * (XA_HEAD_DIM ** -0.5)
        e = jnp.exp(s - jnp.max(s, axis=-1, keepdims=True))
        p = e / jnp.sum(e, axis=-1, keepdims=True)
        oh = _dot(p.astype(BF16), v_ref[0, :, lo:lo + XA_HEAD_DIM])
        oh_scr[:, lo:lo + XA_HEAD_DIM] = oh.astype(BF16)
    o_ref[0] = x + _dot(oh_scr[...], wo_ref[...])


def _mixer_out_cross_attention(x, y, w_mix, b_mix, gain, k, v, w_q, w_o, tq=1024):
    bsz, s, d = x.shape
    m = k.shape[1]
    ky = y.shape[-1]
    tq = min(tq, s)
    mixer_bias = b_mix is not None
    if not mixer_bias:
        b_mix = jnp.zeros((d,), F32)
    return pl.pallas_call(
        functools.partial(_xattn_kernel, mixer_bias=mixer_bias),
        out_shape=jax.ShapeDtypeStruct((bsz, s, d), F32),
        grid=(bsz, s // tq),
        in_specs=[
            pl.BlockSpec((1, tq, d), lambda b, i: (b, i, 0)),
            pl.BlockSpec((1, tq, ky), lambda b, i: (b, i, 0)),
            pl.BlockSpec((ky, d), lambda b, i: (0, 0)),
            pl.BlockSpec((1, d), lambda b, i: (0, 0)),
            pl.BlockSpec((1, d), lambda b, i: (0, 0)),
            pl.BlockSpec((1, m, d), lambda b, i: (b, 0, 0)),
            pl.BlockSpec((1, m, d), lambda b, i: (b, 0, 0)),
            pl.BlockSpec((d, d), lambda b, i: (0, 0)),
            pl.BlockSpec((d, d), lambda b, i: (0, 0)),
        ],
        out_specs=pl.BlockSpec((1, tq, d), lambda b, i: (b, i, 0)),
        scratch_shapes=[pltpu.VMEM((tq, d), BF16)],
        compiler_params=_params("parallel", "parallel"),
        name="cross_attention",
    )(x, y, w_mix, b_mix.reshape(1, d), gain.reshape(1, d), k, v, w_q, w_o)


def _ffn_kernel(xp_ref, x_ref, xq_ref, g_ref, wup_ref, cw_ref, cb_ref, wdn_ref, gf_ref, o_ref,
                xe_scr, h_scr, *, tf, final_norm):
    i = pl.program_id(1)
    last = pl.num_programs(1) - 1
    g = g_ref[...]
    ext = tf + 2 * FFN_HALO
    x = x_ref[0]
    xe_scr[0:FFN_HALO, :] = jnp.where(i > 0, _rms(xp_ref[0], g), 0.0).astype(BF16)
    xe_scr[FFN_HALO:FFN_HALO + tf, :] = _rms(x, g).astype(BF16)
    xe_scr[FFN_HALO + tf:ext, :] = jnp.where(i < last, _rms(xq_ref[0], g), 0.0).astype(BF16)
    for c in range(D_FF // FFN_FC):
        lo = c * FFN_FC
        ge = _dot(xe_scr[...], wup_ref[:, lo:lo + FFN_FC])
        val = _dot(xe_scr[FFN_HALO:FFN_HALO + tf, :], wup_ref[:, D_FF + lo:D_FF + lo + FFN_FC])
        gm1 = pltpu.roll(ge, 1, 0)[FFN_HALO:FFN_HALO + tf]
        g0 = ge[FFN_HALO:FFN_HALO + tf]
        gp1 = pltpu.roll(ge, ext - 1, 0)[FFN_HALO:FFN_HALO + tf]
        w = cw_ref[:, lo:lo + FFN_FC]
        gc = (gm1 * w[0:1] + g0 * w[1:2] + gp1 * w[2:3]) + cb_ref[:, lo:lo + FFN_FC]
        h_scr[:, lo:lo + FFN_FC] = (jax.nn.gelu(gc) * val).astype(BF16)
    y = x + _dot(h_scr[...], wdn_ref[...])
    if final_norm:
        y = _rms(y, gf_ref[...])
    o_ref[0] = y


def _conv_ffn(x, gain, w_up, conv_w, conv_b, w_down, final_gain, final_norm, tf=1024):
    bsz, s, d = x.shape
    tf = min(tf, s)
    nt = s // tf
    hpt = tf // FFN_HALO
    nhb = s // FFN_HALO
    const = dict(pipeline_mode=pl.Buffered(1))
    return pl.pallas_call(
        functools.partial(_ffn_kernel, tf=tf, final_norm=final_norm),
        out_shape=jax.ShapeDtypeStruct((bsz, s, d), F32),
        grid=(bsz, nt),
        in_specs=[
            pl.BlockSpec((1, FFN_HALO, d), lambda b, i: (b, jnp.maximum(i * hpt - 1, 0), 0)),
            pl.BlockSpec((1, tf, d), lambda b, i: (b, i, 0)),
            pl.BlockSpec((1, FFN_HALO, d), lambda b, i: (b, jnp.minimum((i + 1) * hpt, nhb - 1), 0)),
            pl.BlockSpec((1, d), lambda b, i: (0, 0)),
            pl.BlockSpec((d, 2 * D_FF), lambda b, i: (0, 0), **const),
            pl.BlockSpec((3, D_FF), lambda b, i: (0, 0)),
            pl.BlockSpec((1, D_FF), lambda b, i: (0, 0)),
            pl.BlockSpec((D_FF, d), lambda b, i: (0, 0), **const),
            pl.BlockSpec((1, d), lambda b, i: (0, 0)),
        ],
        out_specs=pl.BlockSpec((1, tf, d), lambda b, i: (b, i, 0)),
        scratch_shapes=[
            pltpu.VMEM((tf + 2 * FFN_HALO, d), BF16),
            pltpu.VMEM((tf, D_FF), BF16),
        ],
        compiler_params=_params("parallel", "parallel"),
        name="conv_ffn",
    )(x, x, x, gain.reshape(1, d), w_up, conv_w, conv_b.reshape(1, D_FF), w_down,
      final_gain.reshape(1, d))


def _prepare(p):
    q = dict(p)
    for name in ("fnet_w_out", "lru_w_in", "lru_w_out", "xa_w_q", "xa_w_kv", "xa_w_o",
                 "ffn_w_up", "ffn_w_down"):
        q[name] = p[name].astype(BF16)
    nblk = D_RNN // LRU_BLOCK
    w_a, w_i = p["lru_w_a"], p["lru_w_i"]
    wg = jnp.concatenate([w_a[:, 0], w_a[:, 1], w_i[:, 0], w_i[:, 1]], axis=-1)
    q["lru_wg"] = (0.5 * wg).astype(BF16)

    def per_block(v):
        v = v.reshape(v.shape[0], 2, nblk, LRU_BLOCK)
        return jnp.concatenate([v[:, 0], v[:, 1]], axis=-1)[:, :, None, :]

    q["lru_bg"] = 0.5 * jnp.concatenate([per_block(p["lru_b_a"]), per_block(p["lru_b_i"])], axis=-1)
    q["lru_nls"] = per_block(0.5 * (-LRU_C * jax.nn.softplus(-p["lru_lambda"].astype(F32))))
    return q


def _trunk(x, mem, p, tables):
    depth = p["norm_mix"].shape[0]
    cw, m = tables
    for i in range(depth):
        j = i // 2
        if i % 2 == 0:
            y = _fnet_mix(x, p["norm_mix"][i], cw, m)
            w_mix, b_mix = p["fnet_w_out"][j], p["fnet_b_out"][j]
        else:
            y = _lru_mix(x, p["norm_mix"][i], p["lru_w_in"][j], p["lru_conv_w"][j], p["lru_conv_b"][j],
                         p["lru_wg"][j], p["lru_bg"][j], p["lru_nls"][j])
            w_mix, b_mix = p["lru_w_out"][j], None
        k, v = _kv_proj(mem, p["norm_mem"][i], p["xa_w_kv"][i])
        x = _mixer_out_cross_attention(x, y, w_mix, b_mix, p["norm_xa"][i], k, v,
                                       p["xa_w_q"][i], p["xa_w_o"][i])
        x = _conv_ffn(x, p["norm_ffn"][i], p["ffn_w_up"][i], p["ffn_conv_w"][i], p["ffn_conv_b"][i],
                      p["ffn_w_down"][i], p["norm_final"], final_norm=(i == depth - 1))
    return x


def kernel(x_prompt, x_sample, mem_prompt, mem_sample, norm_mix, fnet_w_out, fnet_b_out, lru_w_in, lru_conv_w, lru_conv_b, lru_w_a, lru_b_a, lru_w_i, lru_b_i, lru_lambda, lru_w_out, norm_xa, norm_mem, xa_w_q, xa_w_kv, xa_w_o, norm_ffn, ffn_w_up, ffn_conv_w, ffn_conv_b, ffn_w_down, norm_final):
    params = _prepare({
        "norm_mix": norm_mix, "fnet_w_out": fnet_w_out, "fnet_b_out": fnet_b_out,
        "lru_w_in": lru_w_in, "lru_conv_w": lru_conv_w, "lru_conv_b": lru_conv_b,
        "lru_w_a": lru_w_a, "lru_b_a": lru_b_a, "lru_w_i": lru_w_i, "lru_b_i": lru_b_i,
        "lru_lambda": lru_lambda, "lru_w_out": lru_w_out,
        "norm_xa": norm_xa, "norm_mem": norm_mem, "xa_w_q": xa_w_q, "xa_w_kv": xa_w_kv, "xa_w_o": xa_w_o,
        "norm_ffn": norm_ffn, "ffn_w_up": ffn_w_up, "ffn_conv_w": ffn_conv_w, "ffn_conv_b": ffn_conv_b,
        "ffn_w_down": ffn_w_down, "norm_final": norm_final,
    })
    outs = []
    for x, mem in ((x_prompt, mem_prompt), (x_sample, mem_sample)):
        cw, m = _fnet_tables(x.shape[1])
        tables = (jnp.asarray(cw).astype(BF16), jnp.asarray(m).astype(BF16))
        outs.append(_trunk(x, mem, params, tables))
    return tuple(outs)
```

```python
import functools

import numpy as np
import jax
import jax.numpy as jnp
from jax import lax
from jax.experimental import pallas as pl
from jax.experimental.pallas import tpu as pltpu

F32 = jnp.float32
BF16 = jnp.bfloat16

D_MODEL = 1024
EPS = 1e-6
FNET_GROUPS = 4
FNET_W = D_MODEL // FNET_GROUPS
FNET_RADIX = 4
D_RNN = 1280
LRU_BLOCK = 128
LRU_CB = 256
LRU_C = 8.0
LRU_SEGS = 8
LRU_STEPS = 8
LRU_CHUNK_ROWS = 256
LRU_BIAS_ROWS = 2
F32_TINY = float(np.finfo(np.float32).tiny)
N_MEM = 256
XA_HEADS = 4
XA_HEAD_DIM = D_MODEL // XA_HEADS
D_FF = 2816
FFN_FC = 256
FFN_HALO = 16

LANES = 128
V7X_VMEM_LIMIT_BYTES =56 * 1024 * 1024


def _rms(x, g):
    ms = jnp.mean(x * x, axis=-1, keepdims=True)
    return (x * lax.rsqrt(ms + EPS)) * g


def _dot(a, b):
    return jnp.dot(a, b, preferred_element_type=F32)


def _params(*sem):
    return pltpu.CompilerParams(dimension_semantics=sem, vmem_limit_bytes=V7X_VMEM_LIMIT_BYTES)


def _fnet_kernel(x_ref, g_ref, cw_ref, m_ref, f_ref, xn_scr, u_scr, y_scr, *, seq):
    q = seq // FNET_RADIX
    rows = min(512, seq)
    for r0 in range(0, seq, rows):
        xn_scr[r0:r0 + rows, :] = _rms(x_ref[0, r0:r0 + rows, :], g_ref[...]).astype(BF16)

    for g in range(FNET_GROUPS):
        slot = g % 2
        z = _dot(xn_scr[:, g * FNET_W:(g + 1) * FNET_W], cw_ref[...])
        zr = [z[j * q:(j + 1) * q, :FNET_W] for j in range(4)]
        zi = [z[j * q:(j + 1) * q, FNET_W:] for j in range(4)]
        t0r, t0i = zr[0] + zr[2], zi[0] + zi[2]
        t1r, t1i = zr[0] - zr[2], zi[0] - zi[2]
        t2r, t2i = zr[1] + zr[3], zi[1] + zi[3]
        t3r, t3i = zr[1] - zr[3], zi[1] - zi[3]
        u = [(t0r + t2r, t0i + t2i), (t1r + t3i, t1i - t3r),
             (t0r - t2r, t0i - t2i), (t1r - t3i, t1i + t3r)]
        for r in range(4):
            u_scr[slot, r, :q, :] = u[r][0].astype(BF16)
            u_scr[slot, r, q:, :] = u[r][1].astype(BF16)
        for r in range(4):
            yr = _dot(m_ref[r], u_scr[slot, r])
            for h in range(FNET_W // LANES):
                y_scr.at[slot, h][pl.ds(r, q, stride=FNET_RADIX), :] = yr[:, h * LANES:(h + 1) * LANES]
        for h in range(FNET_W // LANES):
            lo = g * FNET_W + h * LANES
            f_ref[0, :, lo:lo + LANES] = y_scr[slot, h].astype(BF16)


def _fnet_tables(seq):
    w = FNET_W
    cw_idx = np.outer(np.arange(w), np.arange(w)) % w
    ang = 2.0 * np.pi * cw_idx / w
    cw = np.concatenate([np.cos(ang), -np.sin(ang)], axis=1) / np.sqrt(w)
    q = seq // FNET_RADIX
    mats = []
    for r in range(FNET_RADIX):
        kn = np.outer(FNET_RADIX * np.arange(q) + r, np.arange(q)) % seq
        ang = 2.0 * np.pi * kn / seq
        mats.append(np.concatenate([np.cos(ang), np.sin(ang)], axis=1) / np.sqrt(seq))
    m = np.stack(mats)
    return cw.astype(np.float32), m.astype(np.float32)


def _fnet_mix(x, gain, cw, m):
    bsz, s, d = x.shape
    q = s // FNET_RADIX
    return pl.pallas_call(
        functools.partial(_fnet_kernel, seq=s),
        out_shape=jax.ShapeDtypeStruct((bsz, s, d), BF16),
        grid=(bsz,),
        in_specs=[
            pl.BlockSpec((1, s, d), lambda b: (b, 0, 0)),
            pl.BlockSpec((1, d), lambda b: (0, 0)),
            pl.BlockSpec((FNET_W, 2 * FNET_W), lambda b: (0, 0)),
            pl.BlockSpec((FNET_RADIX, q, 2 * q), lambda b: (0, 0, 0), pipeline_mode=pl.Buffered(1)),
        ],
        out_specs=pl.BlockSpec((1, s, d), lambda b: (b, 0, 0)),
        scratch_shapes=[
            pltpu.VMEM((s, d), BF16),
            pltpu.VMEM((2, FNET_RADIX, 2 * q, FNET_W), BF16),
            pltpu.VMEM((2, FNET_W // LANES, s, LANES), F32),
        ],
        compiler_params=_params("parallel"),
        name="fnet_dft",
    )(x, gain.reshape(1, d), cw, m)


def _lru_kernel(x_ref, g_ref, wgate_ref, wrec_ref, cw_ref, cb_ref, wg_ref, nls_ref,
                y_ref, xn_scr, rec_scr, ch_scr, c16_scr, gts_scr, hp_scr, yp_scr, *, seq):
    ci = pl.program_id(1)
    seg = seq // LRU_SEGS
    cb_w = LRU_CB
    nsub = cb_w // LRU_BLOCK
    rc = min(LRU_CHUNK_ROWS, seq)

    def p_tile(d, sub):
        return hp_scr.at[d * nsub + sub]

    def h_tile(d, sub):
        return hp_scr.at[2 * nsub + d * nsub + sub]

    @pl.when(ci == 0)
    def _():
        for i in range(LRU_SEGS):
            xn = _rms(x_ref[0, i * seg:(i + 1) * seg, :], g_ref[...])
            for t in range(D_MODEL // LANES):
                hp_scr.at[t][pl.ds(i, seg, stride=LRU_SEGS), :] = xn[:, t * LANES:(t + 1) * LANES]
        for t in range(D_MODEL // LANES):
            xn_scr[:, t * LANES:(t + 1) * LANES] = hp_scr[t].astype(BF16)
        lane = lax.broadcasted_iota(jnp.int32, (seq, LRU_BLOCK), 1)
        bias_cols = jnp.where(lane < LRU_BIAS_ROWS, 1.0, 0.0).astype(BF16)
        for sub in range(nsub):
            c16_scr[sub, :, LRU_BLOCK:2 * LRU_BLOCK] = bias_cols

    nchunk = seq // rc

    def rec_matmul(q):
        rows = slice(q * rc, (q + 1) * rc)
        rec_scr[rows, :] = _dot(xn_scr[rows, :], wrec_ref[...])

    def edge_down(blk, s=1):
        row = lax.broadcasted_iota(jnp.int32, blk.shape, 0)
        return jnp.where(row < s, 0.0, pltpu.roll(blk, s, 0))

    def edge_up(blk, s=1):
        row = lax.broadcasted_iota(jnp.int32, blk.shape, 0)
        return jnp.where(row >= LRU_SEGS - s, 0.0, pltpu.roll(blk, LRU_SEGS - s, 0))

    def shifted(k, r0):
        if k > 0:
            lo = r0 - 8 * k
            if lo >= 0:
                return rec_scr[lo:lo + rc, :]
            heads = [edge_down(rec_scr[seq - 8 * (k - e):seq - 8 * (k - e) + 8, :]) for e in range(k)]
            return jnp.concatenate(heads + [rec_scr[0:rc - 8 * k, :]], axis=0)
        hi = r0 + 8
        if hi + rc <= seq:
            return rec_scr[hi:hi + rc, :]
        return jnp.concatenate([rec_scr[hi:seq, :], edge_up(rec_scr[0:8, :])], axis=0)

    cw = cw_ref[...]

    def conv(q):
        r0 = q * rc
        c = (shifted(2, r0) * cw[0:1] + shifted(1, r0) * cw[1:2]
             + rec_scr[r0:r0 + rc, :] * cw[2:3] + shifted(-1, r0) * cw[3:4]) + cb_ref[...]
        ch_scr[r0:r0 + rc, :] = 0.5 * c
        c16 = c.astype(BF16)
        for sub in range(nsub):
            c16_scr[sub, r0:r0 + rc, 0:LRU_BLOCK] = c16[:, sub * LRU_BLOCK:(sub + 1) * LRU_BLOCK]

    def gate_matmuls(q):
        rows = slice(q * rc, (q + 1) * rc)
        for sub in range(nsub):
            gts_scr[sub, rows, :] = _dot(c16_scr[sub, rows, :], wg_ref[sub])

    def output_gate(q):
        rows = slice(q * rc, (q + 1) * rc)
        rec_scr[rows, :] = jax.nn.gelu(_dot(xn_scr[rows, :], wgate_ref[...]))

    first = [q for q in (nchunk - 1, 0, 1, nchunk - 2) if 0 <= q < nchunk]
    first = list(dict.fromkeys(first))
    for q in first:
        rec_matmul(q)
    ends_first = list(dict.fromkeys([0, nchunk - 1]))
    for q in ends_first:
        conv(q)
    for q in ends_first:
        gate_matmuls(q)
    for q in range(nchunk):
        if q not in first:
            rec_matmul(q)
    for q in range(nchunk):
        if q not in ends_first:
            conv(q)

    chains = [(d, sub) for d in range(2) for sub in range(nsub)]
    blk_rows = 8 * LRU_STEPS
    nblk = seg // LRU_STEPS

    def scan_block(it, carry):
        new = []
        for (d, sub), (h, p) in zip(chains, carry):
            row0 = (it if d == 0 else nblk - 1 - it) * blk_rows
            rows = slice(row0, row0 + blk_rows)
            ta = jnp.tanh(gts_scr[sub, rows, d * LANES:(d + 1) * LANES])
            ti = jnp.tanh(gts_scr[sub, rows, (2 + d) * LANES:(3 + d) * LANES])
            nlsh = nls_ref[sub, :, d * LANES:(d + 1) * LANES]
            log_a = nlsh * ta + nlsh
            a = jnp.exp(log_a)
            z = jnp.tanh(log_a) * (-1.0 - a * a)
            ch = ch_scr[rows, sub * LANES:(sub + 1) * LANES]
            b = (z * lax.rsqrt(jnp.maximum(z, F32_TINY))) * (ch * ti + ch)
            hs, ps = [None] * LRU_STEPS, [None] * LRU_STEPS
            for k in (range(LRU_STEPS) if d == 0 else reversed(range(LRU_STEPS))):
                ak = a[8 * k:8 * k + 8]
                h = ak * h + b[8 * k:8 * k + 8]
                p = ak * p
                hs[k], ps[k] = h, p
            h_tile(d, sub)[rows, :] = jnp.concatenate(hs, axis=0)
            p_tile(d, sub)[rows, :] = jnp.concatenate(ps, axis=0)
            new.append((h, p))
        return tuple(new)

    zeros = jnp.zeros((8, LANES), F32)
    ones = jnp.ones((8, LANES), F32)
    bpc = nblk // nchunk
    ready = set(ends_first)
    state = tuple((zeros, ones) for _ in chains)
    for q in range(nchunk):
        for k in (q, nchunk - 1 - q):
            if k not in ready:
                gate_matmuls(k)
                ready.add(k)
        output_gate(q)
        for it in range(q * bpc, (q + 1) * bpc):
            state = scan_block(it, state)
    ends = state

    carries = {}
    for (d, sub), (h_end, p_end) in zip(chains, ends):
        move = edge_down if d == 0 else edge_up
        e, q = move(h_end), move(p_end)
        for s in (1, 2, 4):
            e, q = e + q * move(e, s), q * move(q, s)
        carries[(d, sub)] = e

    for r0 in range(0, seq, rc):
        reps = rc // 8
        gate = rec_scr[r0:r0 + rc, :]
        for sub in range(nsub):
            hf = (h_tile(0, sub)[r0:r0 + rc, :]
                  + p_tile(0, sub)[r0:r0 + rc, :] * jnp.tile(carries[(0, sub)], (reps, 1)))
            hb = (h_tile(1, sub)[r0:r0 + rc, :]
                  + p_tile(1, sub)[r0:r0 + rc, :] * jnp.tile(carries[(1, sub)], (reps, 1)))
            yp_scr.at[sub][r0:r0 + rc, :] = (hf + hb) * gate[:, sub * LANES:(sub + 1) * LANES]

    for i in range(LRU_SEGS):
        for sub in range(nsub):
            y_ref[0, i * seg:(i + 1) * seg, sub * LANES:(sub + 1) * LANES] = (
                yp_scr.at[sub][pl.ds(i, seg, stride=LRU_SEGS), :].astype(BF16))


def _lru_mix(x, gain, w_in, conv_w, conv_b, wg, nls):
    bsz, s, d = x.shape
    nb = D_RNN // LRU_CB
    sub = LRU_CB // LRU_BLOCK
    assert d // LANES >= 4 * sub and LRU_BLOCK == LANES
    return pl.pallas_call(
        functools.partial(_lru_kernel, seq=s),
        out_shape=jax.ShapeDtypeStruct((bsz, s, D_RNN), BF16),
        grid=(bsz, nb),
        in_specs=[
            pl.BlockSpec((1, s, d), lambda b, c: (b, 0, 0)),
            pl.BlockSpec((1, d), lambda b, c: (0, 0)),
            pl.BlockSpec((d, LRU_CB), lambda b, c: (0, c)),
            pl.BlockSpec((d, LRU_CB), lambda b, c: (0, nb + c)),
            pl.BlockSpec((4, LRU_CB), lambda b, c: (0, c)),
            pl.BlockSpec((1, LRU_CB), lambda b, c: (0, c)),
            pl.BlockSpec((sub, 2 * LRU_BLOCK, 4 * LRU_BLOCK), lambda b, c: (c, 0, 0)),
            pl.BlockSpec((sub, 1, 2 * LRU_BLOCK), lambda b, c: (c, 0, 0)),
        ],
        out_specs=pl.BlockSpec((1, s, LRU_CB), lambda b, c: (b, 0, c)),
        scratch_shapes=[
            pltpu.VMEM((s, d), BF16),
            pltpu.VMEM((s, LRU_CB), F32),
            pltpu.VMEM((s, LRU_CB), F32),
            pltpu.VMEM((sub, s, 2 * LRU_BLOCK), BF16),
            pltpu.VMEM((sub, s, 4 * LRU_BLOCK), F32),
            pltpu.VMEM((d // LANES, s, LANES), F32),
            pltpu.VMEM((sub, s, LANES), F32),
        ],
        compiler_params=_params("parallel", "arbitrary"),
        name="lru_core",
    )(x, gain.reshape(1, d), w_in, w_in, conv_w, conv_b.reshape(1, D_RNN), wg, nls)


def _kv_kernel(m_ref, g_ref, w_ref, k_ref, v_ref):
    mn = _rms(m_ref[...], g_ref[...]).astype(BF16)
    kv = _dot(mn, w_ref[...])
    k_ref[...] = kv[:, :D_MODEL].astype(BF16)
    v_ref[...] = kv[:, D_MODEL:].astype(BF16)


def _kv_proj(mem, gain, w_kv, tm=512):
    bsz, m, d = mem.shape
    rows = bsz * m
    tm = min(tm, rows)
    k, v = pl.pallas_call(
        _kv_kernel,
        out_shape=(jax.ShapeDtypeStruct((rows, d), BF16), jax.ShapeDtypeStruct((rows, d), BF16)),
        grid=(rows // tm,),
        in_specs=[
            pl.BlockSpec((tm, d), lambda i: (i, 0)),
            pl.BlockSpec((1, d), lambda i: (0, 0)),
            pl.BlockSpec((d, 2 * d), lambda i: (0, 0)),
        ],
        out_specs=(pl.BlockSpec((tm, d), lambda i: (i, 0)), pl.BlockSpec((tm, d), lambda i: (i, 0))),
        compiler_params=_params("parallel"),
        name="kv_proj",
    )(mem.reshape(rows, d), gain.reshape(1, d), w_kv)
    return k.reshape(bsz, m, d), v.reshape(bsz, m, d)


def _xattn_kernel(x_ref, y_ref, wm_ref, bm_ref, g_ref, k_ref, v_ref, wq_ref, wo_ref, o_ref, oh_scr,
                  *, mixer_bias):
    mix = _dot(y_ref[0], wm_ref[...])
    if mixer_bias:
        mix = mix + bm_ref[...]
    x = x_ref[0] + mix
    xn = _rms(x, g_ref[...]).astype(BF16)
    q = _dot(xn, wq_ref[...])
    for h in range(XA_HEADS):
        lo = h * XA_HEAD_DIM
        qh = q[:, lo:lo + XA_HEAD_DIM].astype(BF16)
        kh = k_ref[0, :, lo:lo + XA_HEAD_DIM]
        s = lax.dot_general(qh, kh, (((1,), (1,)), ((), ())), preferred_element_type=F32)
        s = s * (XA_HEAD_DIM ** -0.5)
        e = jnp.exp(s - jnp.max(s, axis=-1, keepdims=True))
        p = e / jnp.sum(e, axis=-1, keepdims=True)
        oh = _dot(p.astype(BF16), v_ref[0, :, lo:lo + XA_HEAD_DIM])
        oh_scr[:, lo:lo + XA_HEAD_DIM] = oh.astype(BF16)
    o_ref[0] = x + _dot(oh_scr[...], wo_ref[...])


def _mixer_out_cross_attention(x, y, w_mix, b_mix, gain, k, v, w_q, w_o, tq=1024):
    bsz, s, d = x.shape
    m = k.shape[1]
    ky = y.shape[-1]
    tq = min(tq, s)
    mixer_bias = b_mix is not None
    if not mixer_bias:
        b_mix = jnp.zeros((d,), F32)
    return pl.pallas_call(
        functools.partial(_xattn_kernel, mixer_bias=mixer_bias),
        out_shape=jax.ShapeDtypeStruct((bsz, s, d), F32),
        grid=(bsz, s // tq),
        in_specs=[
            pl.BlockSpec((1, tq, d), lambda b, i: (b, i, 0)),
            pl.BlockSpec((1, tq, ky), lambda b, i: (b, i, 0)),
            pl.BlockSpec((ky, d), lambda b, i: (0, 0)),
            pl.BlockSpec((1, d), lambda b, i: (0, 0)),
            pl.BlockSpec((1, d), lambda b, i: (0, 0)),
            pl.BlockSpec((1, m, d), lambda b, i: (b, 0, 0)),
            pl.BlockSpec((1, m, d), lambda b, i: (b, 0, 0)),
            pl.BlockSpec((d, d), lambda b, i: (0, 0)),
            pl.BlockSpec((d, d), lambda b, i: (0, 0)),
        ],
        out_specs=pl.BlockSpec((1, tq, d), lambda b, i: (b, i, 0)),
        scratch_shapes=[pltpu.VMEM((tq, d), BF16)],
        compiler_params=_params("parallel", "parallel"),
        name="cross_attention",
    )(x, y, w_mix, b_mix.reshape(1, d), gain.reshape(1, d), k, v, w_q, w_o)


def _ffn_kernel(xp_ref, x_ref, xq_ref, g_ref, wup_ref, cw_ref, cb_ref, wdn_ref, gf_ref, o_ref,
                xe_scr, h_scr, *, tf, final_norm):
    i = pl.program_id(1)
    last = pl.num_programs(1) - 1
    g = g_ref[...]
    ext = tf + 2 * FFN_HALO
    x = x_ref[0]
    xe_scr[0:FFN_HALO, :] = jnp.where(i > 0, _rms(xp_ref[0], g), 0.0).astype(BF16)
    xe_scr[FFN_HALO:FFN_HALO + tf, :] = _rms(x, g).astype(BF16)
    xe_scr[FFN_HALO + tf:ext, :] = jnp.where(i < last, _rms(xq_ref[0], g), 0.0).astype(BF16)
    for c in range(D_FF // FFN_FC):
        lo = c * FFN_FC
        ge = _dot(xe_scr[...], wup_ref[:, lo:lo + FFN_FC])
        val = _dot(xe_scr[FFN_HALO:FFN_HALO + tf, :], wup_ref[:, D_FF + lo:D_FF + lo + FFN_FC])
        gm1 = pltpu.roll(ge, 1, 0)[FFN_HALO:FFN_HALO + tf]
        g0 = ge[FFN_HALO:FFN_HALO + tf]
        gp1 = pltpu.roll(ge, ext - 1, 0)[FFN_HALO:FFN_HALO + tf]
        w = cw_ref[:, lo:lo + FFN_FC]
        gc = (gm1 * w[0:1] + g0 * w[1:2] + gp1 * w[2:3]) + cb_ref[:, lo:lo + FFN_FC]
        h_scr[:, lo:lo + FFN_FC] = (jax.nn.gelu(gc) * val).astype(BF16)
    y = x + _dot(h_scr[...], wdn_ref[...])
    if final_norm:
        y = _rms(y, gf_ref[...])
    o_ref[0] = y


def _conv_ffn(x, gain, w_up, conv_w, conv_b, w_down, final_gain, final_norm, tf=512):
    bsz, s, d = x.shape
    tf = min(tf, s)
    nt = s // tf
    hpt = tf // FFN_HALO
    nhb = s // FFN_HALO
    const = dict(pipeline_mode=pl.Buffered(1))
    return pl.pallas_call(
        functools.partial(_ffn_kernel, tf=tf, final_norm=final_norm),
        out_shape=jax.ShapeDtypeStruct((bsz, s, d), F32),
        grid=(bsz, nt),
        in_specs=[
            pl.BlockSpec((1, FFN_HALO, d), lambda b, i: (b, jnp.maximum(i * hpt - 1, 0), 0)),
            pl.BlockSpec((1, tf, d), lambda b, i: (b, i, 0)),
            pl.BlockSpec((1, FFN_HALO, d), lambda b, i: (b, jnp.minimum((i + 1) * hpt, nhb - 1), 0)),
            pl.BlockSpec((1, d), lambda b, i: (0, 0)),
            pl.BlockSpec((d, 2 * D_FF), lambda b, i: (0, 0), **const),
            pl.BlockSpec((3, D_FF), lambda b, i: (0, 0)),
            pl.BlockSpec((1, D_FF), lambda b, i: (0, 0)),
            pl.BlockSpec((D_FF, d), lambda b, i: (0, 0), **const),
            pl.BlockSpec((1, d), lambda b, i: (0, 0)),
        ],
        out_specs=pl.BlockSpec((1, tf, d), lambda b, i: (b, i, 0)),
        scratch_shapes=[
            pltpu.VMEM((tf + 2 * FFN_HALO, d), BF16),
            pltpu.VMEM((tf, D_FF), BF16),
        ],
        compiler_params=_params("parallel", "parallel"),
        name="conv_ffn",
    )(x, x, x, gain.reshape(1, d), w_up, conv_w, conv_b.reshape(1, D_FF), w_down,
      final_gain.reshape(1, d))


def _prepare(p):
    q = dict(p)
    for name in ("fnet_w_out", "lru_w_in", "lru_w_out", "xa_w_q", "xa_w_kv", "xa_w_o",
                 "ffn_w_up", "ffn_w_down"):
        q[name] = [p[name][i].astype(BF16) for i in range(p[name].shape[0])]
    nblk = D_RNN // LRU_BLOCK
    w_a, w_i = p["lru_w_a"], p["lru_w_i"]
    wg = (0.5 * jnp.concatenate([w_a[:, 0], w_a[:, 1], w_i[:, 0], w_i[:, 1]], axis=-1)).astype(BF16)

    def per_block(v):
        v = v.reshape(v.shape[0], 2, nblk, LRU_BLOCK)
        return jnp.concatenate([v[:, 0], v[:, 1]], axis=-1)[:, :, None, :]

    bg = 0.5 * jnp.concatenate([per_block(p["lru_b_a"]), per_block(p["lru_b_i"])], axis=-1)
    bg_hi = bg.astype(BF16)
    bg_lo = (bg - bg_hi.astype(F32)).astype(BF16)
    pad = jnp.zeros(wg.shape[:2] + (LRU_BLOCK - LRU_BIAS_ROWS, wg.shape[-1]), BF16)
    q["lru_wg"] = jnp.concatenate([wg, bg_hi, bg_lo, pad], axis=2)
    q["lru_nls"] = per_block(0.5 * (-LRU_C * jax.nn.softplus(-p["lru_lambda"].astype(F32))))
    return q


def _trunk(x, mem, p, tables):
    depth = p["norm_mix"].shape[0]
    cw, m = tables
    for i in range(depth):
        j = i // 2
        if i % 2 == 0:
            y = _fnet_mix(x, p["norm_mix"][i], cw, m)
            w_mix, b_mix = p["fnet_w_out"][j], p["fnet_b_out"][j]
        else:
            y = _lru_mix(x, p["norm_mix"][i], p["lru_w_in"][j], p["lru_conv_w"][j], p["lru_conv_b"][j],
                         p["lru_wg"][j], p["lru_nls"][j])
            w_mix, b_mix = p["lru_w_out"][j], None
        k, v = _kv_proj(mem, p["norm_mem"][i], p["xa_w_kv"][i])
        x = _mixer_out_cross_attention(x, y, w_mix, b_mix, p["norm_xa"][i], k, v,
                                       p["xa_w_q"][i], p["xa_w_o"][i])
        x = _conv_ffn(x, p["norm_ffn"][i], p["ffn_w_up"][i], p["ffn_conv_w"][i], p["ffn_conv_b"][i],
                      p["ffn_w_down"][i], p["norm_final"], final_norm=(i == depth - 1))
    return x


def kernel(x_prompt, x_sample, mem_prompt, mem_sample, norm_mix, fnet_w_out, fnet_b_out, lru_w_in, lru_conv_w, lru_conv_b, lru_w_a, lru_b_a, lru_w_i, lru_b_i, lru_lambda, lru_w_out, norm_xa, norm_mem, xa_w_q, xa_w_kv, xa_w_o, norm_ffn, ffn_w_up, ffn_conv_w, ffn_conv_b, ffn_w_down, norm_final):
    params = _prepare({
        "norm_mix": norm_mix, "fnet_w_out": fnet_w_out, "fnet_b_out": fnet_b_out,
        "lru_w_in": lru_w_in, "lru_conv_w": lru_conv_w, "lru_conv_b": lru_conv_b,
        "lru_w_a": lru_w_a, "lru_b_a": lru_b_a, "lru_w_i": lru_w_i, "lru_b_i": lru_b_i,
        "lru_lambda": lru_lambda, "lru_w_out": lru_w_out,
        "norm_xa": norm_xa, "norm_mem": norm_mem, "xa_w_q": xa_w_q, "xa_w_kv": xa_w_kv, "xa_w_o": xa_w_o,
        "norm_ffn": norm_ffn, "ffn_w_up": ffn_w_up, "ffn_conv_w": ffn_conv_w, "ffn_conv_b": ffn_conv_b,
        "ffn_w_down": ffn_w_down, "norm_final": norm_final,
    })
    outs = []
    for x, mem in ((x_prompt, mem_prompt), (x_sample, mem_sample)):
        cw, m = _fnet_tables(x.shape[1])
        tables = (jnp.asarray(cw).astype(BF16), jnp.asarray(m).astype(BF16))
        outs.append(_trunk(x, mem, params, tables))
    return tuple(outs)
```

```python
import functools

import numpy as np
import jax
import jax.numpy as jnp
from jax import lax
from jax.experimental import pallas as pl
from jax.experimental.pallas import tpu as pltpu

F32 = jnp.float32
BF16 = jnp.bfloat16

D_MODEL = 1024
EPS = 1e-6
FNET_GROUPS = 4
FNET_W = D_MODEL // FNET_GROUPS
FNET_RADIX = 4
D_RNN = 1280
LRU_BLOCK = 128
LRU_CB = 256
LRU_C = 8.0
LRU_SEGS = 8
LRU_STEPS = 8
LRU_CHUNK_ROWS = 128
LRU_BIAS_ROWS = 2
F32_TINY = float(np.finfo(np.float32).tiny)
N_MEM = 256
XA_HEADS = 4
XA_HEAD_DIM = D_MODEL // XA_HEADS
D_FF = 2816
FFN_FC = 256
FFN_HALO = 16

LANES = 128
V7X_VMEM_LIMIT_BYTES =56 * 1024 * 1024


def _rms(x, g):
    ms = jnp.mean(x * x, axis=-1, keepdims=True)
    return (x * lax.rsqrt(ms + EPS)) * g


def _dot(a, b):
    return jnp.dot(a, b, preferred_element_type=F32)


def _params(*sem):
    return pltpu.CompilerParams(dimension_semantics=sem, vmem_limit_bytes=V7X_VMEM_LIMIT_BYTES)


def _fnet_kernel(x_ref, g_ref, cw_ref, m_ref, f_ref, xn_scr, u_scr, y_scr, *, seq):
    q = seq // FNET_RADIX
    rows = min(512, seq)
    for r0 in range(0, seq, rows):
        xn_scr[r0:r0 + rows, :] = _rms(x_ref[0, r0:r0 + rows, :], g_ref[...]).astype(BF16)

    for g in range(FNET_GROUPS):
        slot = g % 2
        z = _dot(xn_scr[:, g * FNET_W:(g + 1) * FNET_W], cw_ref[...])
        zr = [z[j * q:(j + 1) * q, :FNET_W] for j in range(4)]
        zi = [z[j * q:(j + 1) * q, FNET_W:] for j in range(4)]
        t0r, t0i = zr[0] + zr[2], zi[0] + zi[2]
        t1r, t1i = zr[0] - zr[2], zi[0] - zi[2]
        t2r, t2i = zr[1] + zr[3], zi[1] + zi[3]
        t3r, t3i = zr[1] - zr[3], zi[1] - zi[3]
        u = [(t0r + t2r, t0i + t2i), (t1r + t3i, t1i - t3r),
             (t0r - t2r, t0i - t2i), (t1r - t3i, t1i + t3r)]
        for r in range(4):
            u_scr[slot, r, :q, :] = u[r][0].astype(BF16)
            u_scr[slot, r, q:, :] = u[r][1].astype(BF16)
        for r in range(4):
            yr = _dot(m_ref[r], u_scr[slot, r])
            for h in range(FNET_W // LANES):
                y_scr.at[slot, h][pl.ds(r, q, stride=FNET_RADIX), :] = yr[:, h * LANES:(h + 1) * LANES]
        for h in range(FNET_W // LANES):
            lo = g * FNET_W + h * LANES
            f_ref[0, :, lo:lo + LANES] = y_scr[slot, h].astype(BF16)


def _fnet_tables(seq):
    w = FNET_W
    cw_idx = np.outer(np.arange(w), np.arange(w)) % w
    ang = 2.0 * np.pi * cw_idx / w
    cw = np.concatenate([np.cos(ang), -np.sin(ang)], axis=1) / np.sqrt(w)
    q = seq // FNET_RADIX
    mats = []
    for r in range(FNET_RADIX):
        kn = np.outer(FNET_RADIX * np.arange(q) + r, np.arange(q)) % seq
        ang = 2.0 * np.pi * kn / seq
        mats.append(np.concatenate([np.cos(ang), np.sin(ang)], axis=1) / np.sqrt(seq))
    m = np.stack(mats)
    return cw.astype(np.float32), m.astype(np.float32)


def _fnet_mix(x, gain, cw, m):
    bsz, s, d = x.shape
    q = s // FNET_RADIX
    return pl.pallas_call(
        functools.partial(_fnet_kernel, seq=s),
        out_shape=jax.ShapeDtypeStruct((bsz, s, d), BF16),
        grid=(bsz,),
        in_specs=[
            pl.BlockSpec((1, s, d), lambda b: (b, 0, 0)),
            pl.BlockSpec((1, d), lambda b: (0, 0)),
            pl.BlockSpec((FNET_W, 2 * FNET_W), lambda b: (0, 0)),
            pl.BlockSpec((FNET_RADIX, q, 2 * q), lambda b: (0, 0, 0), pipeline_mode=pl.Buffered(1)),
        ],
        out_specs=pl.BlockSpec((1, s, d), lambda b: (b, 0, 0)),
        scratch_shapes=[
            pltpu.VMEM((s, d), BF16),
            pltpu.VMEM((2, FNET_RADIX, 2 * q, FNET_W), BF16),
            pltpu.VMEM((2, FNET_W // LANES, s, LANES), F32),
        ],
        compiler_params=_params("parallel"),
        name="fnet_dft",
    )(x, gain.reshape(1, d), cw, m)


def _lru_kernel(x_ref, g_ref, wgate_ref, wrec_ref, cw_ref, cb_ref, wg_ref, nls_ref,
                y_ref, xn_scr, rec_scr, ch_scr, c16_scr, gts_scr, hp_scr, yp_scr, *, seq):
    ci = pl.program_id(1)
    seg = seq // LRU_SEGS
    cb_w = LRU_CB
    nsub = cb_w // LRU_BLOCK
    rc = min(LRU_CHUNK_ROWS, seq)

    def p_tile(d, sub):
        return hp_scr.at[d * nsub + sub]

    def h_tile(d, sub):
        return hp_scr.at[2 * nsub + d * nsub + sub]

    @pl.when(ci == 0)
    def _():
        for i in range(LRU_SEGS):
            xn = _rms(x_ref[0, i * seg:(i + 1) * seg, :], g_ref[...])
            for t in range(D_MODEL // LANES):
                hp_scr.at[t][pl.ds(i, seg, stride=LRU_SEGS), :] = xn[:, t * LANES:(t + 1) * LANES]
        for t in range(D_MODEL // LANES):
            xn_scr[:, t * LANES:(t + 1) * LANES] = hp_scr[t].astype(BF16)
        lane = lax.broadcasted_iota(jnp.int32, (seq, LRU_BLOCK), 1)
        bias_cols = jnp.where(lane < LRU_BIAS_ROWS, 1.0, 0.0).astype(BF16)
        for sub in range(nsub):
            c16_scr[sub, :, LRU_BLOCK:2 * LRU_BLOCK] = bias_cols

    nchunk = seq // rc

    def rec_matmul(q):
        rows = slice(q * rc, (q + 1) * rc)
        rec_scr[rows, :] = _dot(xn_scr[rows, :], wrec_ref[...])

    def edge_down(blk, s=1):
        row = lax.broadcasted_iota(jnp.int32, blk.shape, 0)
        return jnp.where(row < s, 0.0, pltpu.roll(blk, s, 0))

    def edge_up(blk, s=1):
        row = lax.broadcasted_iota(jnp.int32, blk.shape, 0)
        return jnp.where(row >= LRU_SEGS - s, 0.0, pltpu.roll(blk, LRU_SEGS - s, 0))

    def shifted(k, r0):
        if k > 0:
            lo = r0 - 8 * k
            if lo >= 0:
                return rec_scr[lo:lo + rc, :]
            heads = [edge_down(rec_scr[seq - 8 * (k - e):seq - 8 * (k - e) + 8, :]) for e in range(k)]
            return jnp.concatenate(heads + [rec_scr[0:rc - 8 * k, :]], axis=0)
        hi = r0 + 8
        if hi + rc <= seq:
            return rec_scr[hi:hi + rc, :]
        return jnp.concatenate([rec_scr[hi:seq, :], edge_up(rec_scr[0:8, :])], axis=0)

    cw = cw_ref[...]

    def conv(q):
        r0 = q * rc
        c = (shifted(2, r0) * cw[0:1] + shifted(1, r0) * cw[1:2]
             + rec_scr[r0:r0 + rc, :] * cw[2:3] + shifted(-1, r0) * cw[3:4]) + cb_ref[...]
        ch_scr[r0:r0 + rc, :] = 0.5 * c
        c16 = c.astype(BF16)
        for sub in range(nsub):
            c16_scr[sub, r0:r0 + rc, 0:LRU_BLOCK] = c16[:, sub * LRU_BLOCK:(sub + 1) * LRU_BLOCK]

    def gate_matmuls(q):
        rows = slice(q * rc, (q + 1) * rc)
        for sub in range(nsub):
            gts_scr[sub, rows, :] = _dot(c16_scr[sub, rows, :], wg_ref[sub])

    def output_gate(q):
        rows = slice(q * rc, (q + 1) * rc)
        rec_scr[rows, :] = jax.nn.gelu(_dot(xn_scr[rows, :], wgate_ref[...]))

    first = [q for q in (nchunk - 1, 0, 1, nchunk - 2) if 0 <= q < nchunk]
    first = list(dict.fromkeys(first))
    for q in first:
        rec_matmul(q)
    ends_first = list(dict.fromkeys([0, nchunk - 1]))
    for q in ends_first:
        conv(q)
    for q in ends_first:
        gate_matmuls(q)
    for q in range(nchunk):
        if q not in first:
            rec_matmul(q)
    for q in range(nchunk):
        if q not in ends_first:
            conv(q)

    chains = [(d, sub) for d in range(2) for sub in range(nsub)]
    blk_rows = 8 * LRU_STEPS
    nblk = seg // LRU_STEPS

    def scan_block(it, carry):
        new = []
        for (d, sub), (h, p) in zip(chains, carry):
            row0 = (it if d == 0 else nblk - 1 - it) * blk_rows
            rows = slice(row0, row0 + blk_rows)
            ta = jnp.tanh(gts_scr[sub, rows, d * LANES:(d + 1) * LANES])
            ti = jnp.tanh(gts_scr[sub, rows, (2 + d) * LANES:(3 + d) * LANES])
            nlsh = nls_ref[sub, :, d * LANES:(d + 1) * LANES]
            log_a = nlsh * ta + nlsh
            a = jnp.exp(log_a)
            z = jnp.tanh(log_a) * (-1.0 - a * a)
            ch = ch_scr[rows, sub * LANES:(sub + 1) * LANES]
            b = (z * lax.rsqrt(jnp.maximum(z, F32_TINY))) * (ch * ti + ch)
            hs, ps = [None] * LRU_STEPS, [None] * LRU_STEPS
            for k in (range(LRU_STEPS) if d == 0 else reversed(range(LRU_STEPS))):
                ak = a[8 * k:8 * k + 8]
                h = ak * h + b[8 * k:8 * k + 8]
                p = ak * p
                hs[k], ps[k] = h, p
            h_tile(d, sub)[rows, :] = jnp.concatenate(hs, axis=0)
            p_tile(d, sub)[rows, :] = jnp.concatenate(ps, axis=0)
            new.append((h, p))
        return tuple(new)

    zeros = jnp.zeros((8, LANES), F32)
    ones = jnp.ones((8, LANES), F32)
    bpc = nblk // nchunk
    ready = set(ends_first)
    state = tuple((zeros, ones) for _ in chains)
    for q in range(nchunk):
        for k in (q, nchunk - 1 - q):
            if k not in ready:
                gate_matmuls(k)
                ready.add(k)
        output_gate(q)
        for it in range(q * bpc, (q + 1) * bpc):
            state = scan_block(it, state)
    ends = state

    carries = {}
    for (d, sub), (h_end, p_end) in zip(chains, ends):
        move = edge_down if d == 0 else edge_up
        e, q = move(h_end), move(p_end)
        for s in (1, 2, 4):
            e, q = e + q * move(e, s), q * move(q, s)
        carries[(d, sub)] = e

    for r0 in range(0, seq, rc):
        reps = rc // 8
        gate = rec_scr[r0:r0 + rc, :]
        for sub in range(nsub):
            hf = (h_tile(0, sub)[r0:r0 + rc, :]
                  + p_tile(0, sub)[r0:r0 + rc, :] * jnp.tile(carries[(0, sub)], (reps, 1)))
            hb = (h_tile(1, sub)[r0:r0 + rc, :]
                  + p_tile(1, sub)[r0:r0 + rc, :] * jnp.tile(carries[(1, sub)], (reps, 1)))
            yp_scr.at[sub][r0:r0 + rc, :] = (hf + hb) * gate[:, sub * LANES:(sub + 1) * LANES]

    for i in range(LRU_SEGS):
        for sub in range(nsub):
            y_ref[0, i * seg:(i + 1) * seg, sub * LANES:(sub + 1) * LANES] = (
                yp_scr.at[sub][pl.ds(i, seg, stride=LRU_SEGS), :].astype(BF16))


def _lru_mix(x, gain, w_in, conv_w, conv_b, wg, nls):
    bsz, s, d = x.shape
    nb = D_RNN // LRU_CB
    sub = LRU_CB // LRU_BLOCK
    assert d // LANES >= 4 * sub and LRU_BLOCK == LANES
    return pl.pallas_call(
        functools.partial(_lru_kernel, seq=s),
        out_shape=jax.ShapeDtypeStruct((bsz, s, D_RNN), BF16),
        grid=(bsz, nb),
        in_specs=[
            pl.BlockSpec((1, s, d), lambda b, c: (b, 0, 0)),
            pl.BlockSpec((1, d), lambda b, c: (0, 0)),
            pl.BlockSpec((d, LRU_CB), lambda b, c: (0, c)),
            pl.BlockSpec((d, LRU_CB), lambda b, c: (0, nb + c)),
            pl.BlockSpec((4, LRU_CB), lambda b, c: (0, c)),
            pl.BlockSpec((1, LRU_CB), lambda b, c: (0, c)),
            pl.BlockSpec((sub, 2 * LRU_BLOCK, 4 * LRU_BLOCK), lambda b, c: (c, 0, 0)),
            pl.BlockSpec((sub, 1, 2 * LRU_BLOCK), lambda b, c: (c, 0, 0)),
        ],
        out_specs=pl.BlockSpec((1, s, LRU_CB), lambda b, c: (b, 0, c)),
        scratch_shapes=[
            pltpu.VMEM((s, d), BF16),
            pltpu.VMEM((s, LRU_CB), F32),
            pltpu.VMEM((s, LRU_CB), F32),
            pltpu.VMEM((sub, s, 2 * LRU_BLOCK), BF16),
            pltpu.VMEM((sub, s, 4 * LRU_BLOCK), F32),
            pltpu.VMEM((d // LANES, s, LANES), F32),
            pltpu.VMEM((sub, s, LANES), F32),
        ],
        compiler_params=_params("parallel", "arbitrary"),
        name="lru_core",
    )(x, gain.reshape(1, d), w_in, w_in, conv_w, conv_b.reshape(1, D_RNN), wg, nls)


def _kv_kernel(m_ref, g_ref, w_ref, k_ref, v_ref):
    nb, m, d = m_ref.shape
    mn = _rms(m_ref[...].reshape(nb * m, d), g_ref[...]).astype(BF16)
    kv = _dot(mn, w_ref[...])
    k_ref[...] = kv[:, :d].reshape(nb, m, d).astype(BF16)
    v_ref[...] = kv[:, d:].reshape(nb, m, d).astype(BF16)


def _kv_proj(mem, gain, w_kv):
    bsz, m, d = mem.shape
    nb = 2 if bsz % 2 == 0 else 1
    blk = pl.BlockSpec((nb, m, d), lambda i: (i, 0, 0))
    return pl.pallas_call(
        _kv_kernel,
        out_shape=(jax.ShapeDtypeStruct((bsz, m, d), BF16), jax.ShapeDtypeStruct((bsz, m, d), BF16)),
        grid=(bsz // nb,),
        in_specs=[
            blk,
            pl.BlockSpec((1, d), lambda i: (0, 0)),
            pl.BlockSpec((d, 2 * d), lambda i: (0, 0)),
        ],
        out_specs=(blk, blk),
        compiler_params=_params("parallel"),
        name="kv_proj",
    )(mem, gain.reshape(1, d), w_kv)


def _xattn_kernel(x_ref, y_ref, wm_ref, bm_ref, g_ref, k_ref, v_ref, wq_ref, wo_ref, o_ref, oh_scr,
                  *, mixer_bias):
    mix = _dot(y_ref[0], wm_ref[...])
    if mixer_bias:
        mix = mix + bm_ref[...]
    x = x_ref[0] + mix
    xn = _rms(x, g_ref[...]).astype(BF16)
    q = _dot(xn, wq_ref[...])
    for h in range(XA_HEADS):
        lo = h * XA_HEAD_DIM
        qh = q[:, lo:lo + XA_HEAD_DIM].astype(BF16)
        kh = k_ref[0, :, lo:lo + XA_HEAD_DIM]
        s = lax.dot_general(qh, kh, (((1,), (1,)), ((), ())), preferred_element_type=F32)
        s = s * (XA_HEAD_DIM ** -0.5)
        e = jnp.exp(s - jnp.max(s, axis=-1, keepdims=True))
        p = e / jnp.sum(e, axis=-1, keepdims=True)
        oh = _dot(p.astype(BF16), v_ref[0, :, lo:lo + XA_HEAD_DIM])
        oh_scr[:, lo:lo + XA_HEAD_DIM] = oh.astype(BF16)
    o_ref[0] = x + _dot(oh_scr[...], wo_ref[...])


def _mixer_out_cross_attention(x, y, w_mix, b_mix, gain, k, v, w_q, w_o, tq=1024):
    bsz, s, d = x.shape
    m = k.shape[1]
    ky = y.shape[-1]
    tq = min(tq, s)
    mixer_bias = b_mix is not None
    if not mixer_bias:
        b_mix = jnp.zeros((d,), F32)
    return pl.pallas_call(
        functools.partial(_xattn_kernel, mixer_bias=mixer_bias),
        out_shape=jax.ShapeDtypeStruct((bsz, s, d), F32),
        grid=(bsz, s // tq),
        in_specs=[
            pl.BlockSpec((1, tq, d), lambda b, i: (b, i, 0)),
            pl.BlockSpec((1, tq, ky), lambda b, i: (b, i, 0)),
            pl.BlockSpec((ky, d), lambda b, i: (0, 0)),
            pl.BlockSpec((1, d), lambda b, i: (0, 0)),
            pl.BlockSpec((1, d), lambda b, i: (0, 0)),
            pl.BlockSpec((1, m, d), lambda b, i: (b, 0, 0)),
            pl.BlockSpec((1, m, d), lambda b, i: (b, 0, 0)),
            pl.BlockSpec((d, d), lambda b, i: (0, 0)),
            pl.BlockSpec((d, d), lambda b, i: (0, 0)),
        ],
        out_specs=pl.BlockSpec((1, tq, d), lambda b, i: (b, i, 0)),
        scratch_shapes=[pltpu.VMEM((tq, d), BF16)],
        compiler_params=_params("parallel", "parallel"),
        name="cross_attention",
    )(x, y, w_mix, b_mix.reshape(1, d), gain.reshape(1, d), k, v, w_q, w_o)


def _ffn_kernel(xp_ref, x_ref, xq_ref, g_ref, wup_ref, cw_ref, cb_ref, wdn_ref, gf_ref, o_ref,
                xe_scr, h_scr, *, tf, final_norm):
    i = pl.program_id(1)
    last = pl.num_programs(1) - 1
    g = g_ref[...]
    ext = tf + 2 * FFN_HALO
    x = x_ref[0]
    xe_scr[0:FFN_HALO, :] = jnp.where(i > 0, _rms(xp_ref[0], g), 0.0).astype(BF16)
    xe_scr[FFN_HALO:FFN_HALO + tf, :] = _rms(x, g).astype(BF16)
    xe_scr[FFN_HALO + tf:ext, :] = jnp.where(i < last, _rms(xq_ref[0], g), 0.0).astype(BF16)
    for c in range(D_FF // FFN_FC):
        lo = c * FFN_FC
        ge = _dot(xe_scr[...], wup_ref[:, lo:lo + FFN_FC])
        val = _dot(xe_scr[FFN_HALO:FFN_HALO + tf, :], wup_ref[:, D_FF + lo:D_FF + lo + FFN_FC])
        gm1 = pltpu.roll(ge, 1, 0)[FFN_HALO:FFN_HALO + tf]
        g0 = ge[FFN_HALO:FFN_HALO + tf]
        gp1 = pltpu.roll(ge, ext - 1, 0)[FFN_HALO:FFN_HALO + tf]
        w = cw_ref[:, lo:lo + FFN_FC]
        gc = (gm1 * w[0:1] + g0 * w[1:2] + gp1 * w[2:3]) + cb_ref[:, lo:lo + FFN_FC]
        h_scr[:, lo:lo + FFN_FC] = (jax.nn.gelu(gc) * val).astype(BF16)
    y = x + _dot(h_scr[...], wdn_ref[...])
    if final_norm:
        y = _rms(y, gf_ref[...])
    o_ref[0] = y


def _conv_ffn(x, gain, w_up, conv_w, conv_b, w_down, final_gain, final_norm, tf=512):
    bsz, s, d = x.shape
    tf = min(tf, s)
    nt = s // tf
    hpt = tf // FFN_HALO
    nhb = s // FFN_HALO
    const = dict(pipeline_mode=pl.Buffered(1))
    return pl.pallas_call(
        functools.partial(_ffn_kernel, tf=tf, final_norm=final_norm),
        out_shape=jax.ShapeDtypeStruct((bsz, s, d), F32),
        grid=(bsz, nt),
        in_specs=[
            pl.BlockSpec((1, FFN_HALO, d), lambda b, i: (b, jnp.maximum(i * hpt - 1, 0), 0)),
            pl.BlockSpec((1, tf, d), lambda b, i: (b, i, 0)),
            pl.BlockSpec((1, FFN_HALO, d), lambda b, i: (b, jnp.minimum((i + 1) * hpt, nhb - 1), 0)),
            pl.BlockSpec((1, d), lambda b, i: (0, 0)),
            pl.BlockSpec((d, 2 * D_FF), lambda b, i: (0, 0), **const),
            pl.BlockSpec((3, D_FF), lambda b, i: (0, 0)),
            pl.BlockSpec((1, D_FF), lambda b, i: (0, 0)),
            pl.BlockSpec((D_FF, d), lambda b, i: (0, 0), **const),
            pl.BlockSpec((1, d), lambda b, i: (0, 0)),
        ],
        out_specs=pl.BlockSpec((1, tf, d), lambda b, i: (b, i, 0)),
        scratch_shapes=[
            pltpu.VMEM((tf + 2 * FFN_HALO, d), BF16),
            pltpu.VMEM((tf, D_FF), BF16),
        ],
        compiler_params=_params("parallel", "parallel"),
        name="conv_ffn",
    )(x, x, x, gain.reshape(1, d), w_up, conv_w, conv_b.reshape(1, D_FF), w_down,
      final_gain.reshape(1, d))


def _prepare(p):
    q = dict(p)
    for name in ("fnet_w_out", "lru_w_in", "lru_w_out", "xa_w_q", "xa_w_kv", "xa_w_o",
                 "ffn_w_up", "ffn_w_down"):
        q[name] = [p[name][i].astype(BF16) for i in range(p[name].shape[0])]
    nblk = D_RNN // LRU_BLOCK
    w_a, w_i = p["lru_w_a"], p["lru_w_i"]
    wg = (0.5 * jnp.concatenate([w_a[:, 0], w_a[:, 1], w_i[:, 0], w_i[:, 1]], axis=-1)).astype(BF16)

    def per_block(v):
        v = v.reshape(v.shape[0], 2, nblk, LRU_BLOCK)
        return jnp.concatenate([v[:, 0], v[:, 1]], axis=-1)[:, :, None, :]

    bg = 0.5 * jnp.concatenate([per_block(p["lru_b_a"]), per_block(p["lru_b_i"])], axis=-1)
    bg_hi = bg.astype(BF16)
    bg_lo = (bg - bg_hi.astype(F32)).astype(BF16)
    pad = jnp.zeros(wg.shape[:2] + (LRU_BLOCK - LRU_BIAS_ROWS, wg.shape[-1]), BF16)
    q["lru_wg"] = jnp.concatenate([wg, bg_hi, bg_lo, pad], axis=2)
    q["lru_nls"] = per_block(0.5 * (-LRU_C * jax.nn.softplus(-p["lru_lambda"].astype(F32))))
    return q


def _trunk(x, mem, p, tables):
    depth = p["norm_mix"].shape[0]
    cw, m = tables
    for i in range(depth):
        j = i // 2
        if i % 2 == 0:
            y = _fnet_mix(x, p["norm_mix"][i], cw, m)
            w_mix, b_mix = p["fnet_w_out"][j], p["fnet_b_out"][j]
        else:
            y = _lru_mix(x, p["norm_mix"][i], p["lru_w_in"][j], p["lru_conv_w"][j], p["lru_conv_b"][j],
                         p["lru_wg"][j], p["lru_nls"][j])
            w_mix, b_mix = p["lru_w_out"][j], None
        k, v = _kv_proj(mem, p["norm_mem"][i], p["xa_w_kv"][i])
        x = _mixer_out_cross_attention(x, y, w_mix, b_mix, p["norm_xa"][i], k, v,
                                       p["xa_w_q"][i], p["xa_w_o"][i])
        x = _conv_ffn(x, p["norm_ffn"][i], p["ffn_w_up"][i], p["ffn_conv_w"][i], p["ffn_conv_b"][i],
                      p["ffn_w_down"][i], p["norm_final"], final_norm=(i == depth - 1))
    return x


def kernel(x_prompt, x_sample, mem_prompt, mem_sample, norm_mix, fnet_w_out, fnet_b_out, lru_w_in, lru_conv_w, lru_conv_b, lru_w_a, lru_b_a, lru_w_i, lru_b_i, lru_lambda, lru_w_out, norm_xa, norm_mem, xa_w_q, xa_w_kv, xa_w_o, norm_ffn, ffn_w_up, ffn_conv_w, ffn_conv_b, ffn_w_down, norm_final):
    params = _prepare({
        "norm_mix": norm_mix, "fnet_w_out": fnet_w_out, "fnet_b_out": fnet_b_out,
        "lru_w_in": lru_w_in, "lru_conv_w": lru_conv_w, "lru_conv_b": lru_conv_b,
        "lru_w_a": lru_w_a, "lru_b_a": lru_b_a, "lru_w_i": lru_w_i, "lru_b_i": lru_b_i,
        "lru_lambda": lru_lambda, "lru_w_out": lru_w_out,
        "norm_xa": norm_xa, "norm_mem": norm_mem, "xa_w_q": xa_w_q, "xa_w_kv": xa_w_kv, "xa_w_o": xa_w_o,
        "norm_ffn": norm_ffn, "ffn_w_up": ffn_w_up, "ffn_conv_w": ffn_conv_w, "ffn_conv_b": ffn_conv_b,
        "ffn_w_down": ffn_w_down, "norm_final": norm_final,
    })
    outs = []
    for x, mem in ((x_prompt, mem_prompt), (x_sample, mem_sample)):
        cw, m = _fnet_tables(x.shape[1])
        tables = (jnp.asarray(cw).astype(BF16), jnp.asarray(m).astype(BF16))
        outs.append(_trunk(x, mem, params, tables))
    return tuple(outs)
```
